```python
import math
import jax, jax.numpy as jnp
from jax import lax
import numpy as np

D_MODEL = 2048
BATCH = 4
SEQ = 2048
DEPTH = 2

CTX_LEN = 256
GRID_W = 64
N_MIXERS = 2
N_GLA_LAYERS = (DEPTH + 1) // 2
N_CONV_LAYERS = DEPTH // 2
N_MOD = 9
D_FF = ((8 * D_MODEL // 3 + 127) // 128) * 128
GLA_HEADS = 4
GLA_DK = D_MODEL // 2
GLA_DV = D_MODEL
GLA_HEAD_K = GLA_DK // GLA_HEADS
GLA_HEAD_V = GLA_DV // GLA_HEADS
GLA_GATE_RANK = 16
GLA_GATE_TEMP = 16.0
GLA_CHUNK = 64
GLA_QR_DIM = GLA_DK + GLA_DV
GLA_IN_DIM = 2 * GLA_DK + 2 * GLA_DV + 2 * GLA_GATE_RANK
CONV_WIDTH = 31
CONV_PAD = CONV_WIDTH // 2
EPS = 1e-6

kernel_name = 'hybrid_gla_conformer_macaron_dit'


def rmsnorm(z, g):
    z32 = z.astype(jnp.float32)
    y = z32 * lax.rsqrt(jnp.mean(z32 * z32, axis=-1, keepdims=True) + EPS)
    return (y * g.astype(jnp.float32)).astype(z.dtype)


def adanorm(z, g, shift, scale):
    return rmsnorm(z, g) * (1 + scale[:, None]) + shift[:, None]


def swiglu(u, w_gate, w_up, w_down):
    return (jax.nn.silu(u @ w_gate) * (u @ w_up)) @ w_down


def ffn_sub(z, m, g, w_gate, w_up, w_down, base):
    u = adanorm(z, g, m[:, base], m[:, base + 1])
    return z + 0.5 * m[:, base + 2][:, None] * swiglu(u, w_gate, w_up, w_down)


def pos_embed_2d(n_tok, dtype):
    rows = n_tok // GRID_W
    rr, cc = jnp.meshgrid(jnp.arange(rows), jnp.arange(GRID_W), indexing='ij')
    rr = rr.reshape(-1).astype(jnp.float32)
    cc = cc.reshape(-1).astype(jnp.float32)
    quarter = D_MODEL // 4
    omega = 1.0 / (10000.0 ** (jnp.arange(quarter, dtype=jnp.float32) / quarter))
    ar = rr[:, None] * omega[None]
    ac = cc[:, None] * omega[None]
    emb = jnp.concatenate([jnp.sin(ar), jnp.cos(ar), jnp.sin(ac), jnp.cos(ac)], axis=-1)
    return emb.astype(dtype)


def to_heads(z):
    b, l, _ = z.shape
    return z.reshape(b, l, GLA_HEADS, -1).transpose(0, 2, 1, 3)


def flip_seq(t):
    return jnp.flip(t, axis=2)


def gla_project(u, w_in, w_dec_f, b_dec_f, w_dec_b, b_dec_b, with_query):
    if with_query:
        p = u @ w_in
        qr, p = p[..., :GLA_QR_DIM], p[..., GLA_QR_DIM:]
        q, r = jnp.split(qr, [GLA_DK], axis=-1)
        q = to_heads(q).astype(jnp.float32) * (GLA_HEAD_K ** -0.5)
    else:
        p = u @ w_in[:, GLA_QR_DIM:]
        q, r = None, None
    k, v, zf, zb = jnp.split(p, [GLA_DK, GLA_DK + GLA_DV, GLA_DK + GLA_DV + GLA_GATE_RANK], axis=-1)
    k = to_heads(k).astype(jnp.float32)
    v = to_heads(v).astype(jnp.float32)
    gf = to_heads(jax.nn.log_sigmoid((zf @ w_dec_f + b_dec_f).astype(jnp.float32)) / GLA_GATE_TEMP)
    gb = to_heads(jax.nn.log_sigmoid((zb @ w_dec_b + b_dec_b).astype(jnp.float32)) / GLA_GATE_TEMP)
    return q, r, k, v, gf, gb


def gla_scan(k, v, g, s0, q=None):
    bsz, nh, length, _ = k.shape
    n_chunks = length // GLA_CHUNK

    def chunks(t):
        return jnp.moveaxis(t.reshape(bsz, nh, n_chunks, GLA_CHUNK, t.shape[-1]), 2, 0)

    def state_update(s, kc, vc, b):
        b_last = b[:, :, -1:, :]
        return (jnp.exp(b_last)[:, :, 0, :, None] * s
                + jnp.einsum('bhsd,bhse->bhde', kc * jnp.exp(b_last - b), vc))

    if q is None:
        def step_state(s, inp):
            kc, vc, gc = inp
            return state_update(s, kc, vc, jnp.cumsum(gc, axis=2)), None
        s_fin, _ = lax.scan(step_state, s0, (chunks(k), chunks(v), chunks(g)))
        return None, s_fin

    mask = jnp.tril(jnp.ones((GLA_CHUNK, GLA_CHUNK), dtype=bool))[:, :, None]

    def step(s, inp):
        qc, kc, vc, gc = inp
        b = jnp.cumsum(gc, axis=2)
        o_inter = jnp.einsum('bhtd,bhde->bhte', qc * jnp.exp(b), s)
        diff = b[:, :, :, None, :] - b[:, :, None, :, :]
        decay = jnp.where(mask, jnp.exp(jnp.where(mask, diff, 0.0)), 0.0)
        scores = jnp.einsum('bhtd,bhsd,bhtsd->bhts', qc, kc, decay)
        o = o_inter + jnp.einsum('bhts,bhse->bhte', scores, vc)
        return state_update(s, kc, vc, b), o

    s_fin, o = lax.scan(step, s0, (chunks(q), chunks(k), chunks(v), chunks(g)))
    o = jnp.moveaxis(o, 0, 2).reshape(bsz, nh, length, -1)
    return o, s_fin


def gla_bidir(k, v, gf, gb, sf0, sb0, q=None):
    of, sf = gla_scan(k, v, gf, sf0, q)
    ob, sb = gla_scan(flip_seq(k), flip_seq(v), flip_seq(gb), sb0,
                      None if q is None else flip_seq(q))
    o = None if q is None else of + flip_seq(ob)
    return o, sf, sb


def gla_output(o, r, out_g, w_out):
    bsz, nh, length, dv = o.shape
    o = o * lax.rsqrt(jnp.mean(o * o, axis=-1, keepdims=True) + EPS)
    o = o * out_g.astype(jnp.float32).reshape(nh, dv)[None, :, None, :]
    o = o.transpose(0, 2, 1, 3).reshape(bsz, length, nh * dv).astype(r.dtype)
    return (o * jax.nn.silu(r)) @ w_out


def gla_mixer(u, uc, need_ctx_out, w_in, w_dec_f, b_dec_f, w_dec_b, b_dec_b, out_g, w_out):
    bsz = u.shape[0]
    zeros = jnp.zeros((bsz, GLA_HEADS, GLA_HEAD_K, GLA_HEAD_V), jnp.float32)
    qc, rc, kc, vc, gfc, gbc = gla_project(uc, w_in, w_dec_f, b_dec_f, w_dec_b, b_dec_b, need_ctx_out)
    oc, sf_ctx, sb_ctx = gla_bidir(kc, vc, gfc, gbc, zeros, zeros, qc)
    yc = gla_output(oc, rc, out_g, w_out) if need_ctx_out else None
    q, r, k, v, gf, gb = gla_project(u, w_in, w_dec_f, b_dec_f, w_dec_b, b_dec_b, True)
    o, _, _ = gla_bidir(k, v, gf, gb, sf_ctx, sb_ctx, q)
    return gla_output(o, r, out_g, w_out), yc


def conv_module(u, w_pw1, b_pw1, w_dw, b_dw, ln_g, ln_b, w_pw2, b_pw2):
    a, gt = jnp.split(u @ w_pw1 + b_pw1, 2, axis=-1)
    z = a * jax.nn.sigmoid(gt)
    z = lax.conv_general_dilated(z, w_dw[:, None, :], window_strides=(1,),
                                 padding=[(CONV_PAD, CONV_PAD)],
                                 dimension_numbers=('NWC', 'WIO', 'NWC'),
                                 feature_group_count=D_MODEL) + b_dw
    z32 = z.astype(jnp.float32)
    mu = jnp.mean(z32, axis=-1, keepdims=True)
    var = jnp.mean(jnp.square(z32 - mu), axis=-1, keepdims=True)
    zn = ((z32 - mu) * lax.rsqrt(var + EPS) * ln_g.astype(jnp.float32)
          + ln_b.astype(jnp.float32)).astype(u.dtype)
    return jax.nn.silu(zn) @ w_pw2 + b_pw2


def setup_inputs(seed: int = 0) -> dict:
    key = jax.random.key(seed)
    ks = jax.random.split(key, 32)
    f32 = jnp.float32
    D, F, R, W = D_MODEL, D_FF, GLA_GATE_RANK, CONV_WIDTH
    NG, NC = N_GLA_LAYERS, N_CONV_LAYERS

    def nrm(k, shape, scale):
        return jax.random.normal(k, shape, f32) * scale

    return {
        'x': nrm(ks[0], (BATCH, SEQ, D), 1.0),
        'c': nrm(ks[1], (BATCH, D), 1.0),
        'ctx': nrm(ks[2], (BATCH, CTX_LEN, D), 1.0),
        'c_ctx': nrm(ks[3], (D,), 1.0),
        'ada_w': nrm(ks[4], (DEPTH, D, N_MOD * D), 0.5 * D ** -0.5),
        'ada_b': nrm(ks[5], (DEPTH, N_MOD * D), 0.01),
        'norm_g': 1.0 + nrm(ks[6], (DEPTH, 3, D), 0.01),
        'final_norm_g': 1.0 + nrm(ks[7], (D,), 0.01),
        'ffn_w_gate': nrm(ks[8], (DEPTH, 2, D, F), D ** -0.5),
        'ffn_w_up': nrm(ks[9], (DEPTH, 2, D, F), D ** -0.5),
        'ffn_w_down': nrm(ks[10], (DEPTH, 2, F, D), F ** -0.5),
        'gla_w_in': nrm(ks[11], (NG, D, GLA_IN_DIM), D ** -0.5),
        'gla_w_decay_f': nrm(ks[12], (NG, R, GLA_DK), R ** -0.5),
        'gla_b_decay_f': nrm(ks[13], (NG, GLA_DK), 0.1),
        'gla_w_decay_b': nrm(ks[14], (NG, R, GLA_DK), R ** -0.5),
        'gla_b_decay_b': nrm(ks[15], (NG, GLA_DK), 0.1),
        'gla_out_norm_g': 1.0 + nrm(ks[16], (NG, GLA_DV), 0.01),
        'gla_w_out': nrm(ks[17], (NG, GLA_DV, D), GLA_DV ** -0.5),
        'conv_w_pw1': nrm(ks[18], (NC, D, 2 * D), D ** -0.5),
        'conv_b_pw1': nrm(ks[19], (NC, 2 * D), 0.01),
        'conv_w_dw': nrm(ks[20], (NC, W, D), W ** -0.5),
        'conv_b_dw': nrm(ks[21], (NC, D), 0.01),
        'conv_ln_g': 1.0 + nrm(ks[22], (NC, D), 0.01),
        'conv_ln_b': nrm(ks[23], (NC, D), 0.01),
        'conv_w_pw2': nrm(ks[24], (NC, D, D), D ** -0.5),
        'conv_b_pw2': nrm(ks[25], (NC, D), 0.01),
    }


def reference(x, c, ctx, c_ctx, ada_w, ada_b, norm_g, final_norm_g, ffn_w_gate, ffn_w_up, ffn_w_down,
              gla_w_in, gla_w_decay_f, gla_b_decay_f, gla_w_decay_b, gla_b_decay_b, gla_out_norm_g,
              gla_w_out, conv_w_pw1, conv_b_pw1, conv_w_dw, conv_b_dw, conv_ln_g, conv_ln_b,
              conv_w_pw2, conv_b_pw2):
    bsz, n_tok, _ = x.shape
    h = x + pos_embed_2d(n_tok, x.dtype)[None]
    hc = ctx
    for i in range(DEPTH):
        kind = i % N_MIXERS
        ctx_later = any(j % N_MIXERS == 0 for j in range(i + 1, DEPTH))
        ctx_here = (kind == 0) or ctx_later
        ml = (jax.nn.silu(c) @ ada_w[i] + ada_b[i]).reshape(bsz, N_MOD, D_MODEL)
        mc = (jax.nn.silu(c_ctx)[None] @ ada_w[i] + ada_b[i]).reshape(1, N_MOD, D_MODEL)
        h = ffn_sub(h, ml, norm_g[i, 0], ffn_w_gate[i, 0], ffn_w_up[i, 0], ffn_w_down[i, 0], 0)
        if ctx_here:
            hc = ffn_sub(hc, mc, norm_g[i, 0], ffn_w_gate[i, 0], ffn_w_up[i, 0], ffn_w_down[i, 0], 0)
        u = adanorm(h, norm_g[i, 1], ml[:, 3], ml[:, 4])
        uc = adanorm(hc, norm_g[i, 1], mc[:, 3], mc[:, 4]) if ctx_here else None
        li = i // N_MIXERS
        if kind == 0:
            y, yc = gla_mixer(u, uc, ctx_later, gla_w_in[li], gla_w_decay_f[li], gla_b_decay_f[li],
                              gla_w_decay_b[li], gla_b_decay_b[li], gla_out_norm_g[li], gla_w_out[li])
        else:
            conv_args = (conv_w_pw1[li], conv_b_pw1[li], conv_w_dw[li], conv_b_dw[li],
                         conv_ln_g[li], conv_ln_b[li], conv_w_pw2[li], conv_b_pw2[li])
            y = conv_module(u, *conv_args)
            yc = conv_module(uc, *conv_args) if ctx_later else None
        h = h + ml[:, 5][:, None] * y
        if ctx_later:
            hc = hc + mc[:, 5][:, None] * yc
            hc = ffn_sub(hc, mc, norm_g[i, 2], ffn_w_gate[i, 1], ffn_w_up[i, 1], ffn_w_down[i, 1], 6)
        h = ffn_sub(h, ml, norm_g[i, 2], ffn_w_gate[i, 1], ffn_w_up[i, 1], ffn_w_down[i, 1], 6)
    return rmsnorm(h, final_norm_g)
```

```python
import functools

import jax
import jax.numpy as jnp
from jax import lax
from jax.experimental import pallas as pl
from jax.experimental.pallas import tpu as pltpu

F32 = jnp.float32
BF16 = jnp.bfloat16

EPS = 1e-6
N_MOD = 9
GRID_W = 64
GLA_HEADS = 4
GLA_GATE_TEMP = 16.0
GLA_CHUNK = 64
GLA_SUB = 16
CONV_ROW_HALO = 16
MOD_ROWS = 8
V7X_VMEM_LIMIT = 56 * 1024 * 1024
LANE = 128


def _cparams(n_axes):
    return pltpu.CompilerParams(
        dimension_semantics=("arbitrary",) * n_axes,
        vmem_limit_bytes=V7X_VMEM_LIMIT)


def _dot(a, b):
    return jnp.dot(a, b, preferred_element_type=F32)


def _dot_nt(a, b):
    return lax.dot_general(a, b, (((1,), (1,)), ((), ())), preferred_element_type=F32)


def _dot_tn(a, b):
    return lax.dot_general(a, b, (((0,), (0,)), ((), ())), preferred_element_type=F32)


def _dot_f32(a, b):
    return jnp.dot(a, b, preferred_element_type=F32, precision=lax.Precision.HIGHEST)


def _silu(x):
    return x * jax.nn.sigmoid(x)


def _adanorm(z, g, shift, scale):
    y = z * lax.rsqrt(jnp.mean(z * z, axis=-1, keepdims=True) + EPS)
    return (y * g) * (1.0 + scale) + shift


def _mod_kernel(c_ref, w_ref, b_ref, o_ref):
    s = _silu(c_ref[...]).astype(BF16)
    o_ref[...] = _dot(s, w_ref[...].astype(BF16)) + b_ref[...]


def _modulation(cond, ada_w, ada_b):
    depth, d, n = ada_w.shape
    tn = _pick_tile(n, 1024)
    return pl.pallas_call(
        _mod_kernel,
        out_shape=jax.ShapeDtypeStruct((depth, MOD_ROWS, n), F32),
        grid=(depth, n // tn),
        in_specs=[
            pl.BlockSpec((MOD_ROWS, d), lambda l, j: (0, 0)),
            pl.BlockSpec((None, d, tn), lambda l, j: (l, 0, j)),
            pl.BlockSpec((None, 1, tn), lambda l, j: (l, 0, j)),
        ],
        out_specs=pl.BlockSpec((None, MOD_ROWS, tn), lambda l, j: (l, 0, j)),
        compiler_params=_cparams(2),
        name="ada_modulation",
    )(cond, ada_w, ada_b.reshape(depth, 1, n))


def _mod_spec(layer, k, row_of_tile, d):
    return pl.BlockSpec((None, None, None, 1, d),
                        lambda i, *_: (layer, row_of_tile(i), k, 0, 0))


def _gain_spec(layer, k, d):
    return pl.BlockSpec((None, None, 1, d), lambda *_: (layer, k, 0, 0))


def _ffn_kernel(*refs, has_pos, final_norm):
    it = iter(refs)
    z_ref = next(it)
    pos_ref = next(it) if has_pos else None
    shift_ref, scale_ref, gate_ref, g_ref = next(it), next(it), next(it), next(it)
    fg_ref = next(it) if final_norm else None
    wg_ref, wu_ref, wd_ref = next(it), next(it), next(it)
    o_ref = next(it)
    zres_ref = next(it) if has_pos else None
    u_scr, acc_scr = next(it), next(it)

    j = pl.program_id(1)

    @pl.when(j == 0)
    def _():
        z = z_ref[...]
        if has_pos:
            z = z + pos_ref[...]
            zres_ref[...] = z
        u = _adanorm(z, g_ref[...], shift_ref[...], scale_ref[...])
        u_scr[...] = u.astype(BF16)
        acc_scr[...] = jnp.zeros_like(acc_scr)

    u = u_scr[...]
    a = _dot(u, wg_ref[...])
    b = _dot(u, wu_ref[...])
    hid = (_silu(a) * b).astype(BF16)
    acc_scr[...] += _dot(hid, wd_ref[...])

    @pl.when(j == pl.num_programs(1) - 1)
    def _():
        z = zres_ref[...] if has_pos else z_ref[...]
        out = z + (0.5 * gate_ref[...]) * acc_scr[...]
        if final_norm:
            out = out * lax.rsqrt(jnp.mean(out * out, axis=-1, keepdims=True) + EPS) * fg_ref[...]
        o_ref[...] = out


def _ffn(z, mod, gains, wg, wu, wd, *, layer, which, row_of_tile, tm, tf,
         pos=None, final_gain=None):
    t, d = z.shape
    fp = wg.shape[-1]
    base = 0 if which == 0 else 6
    gain_k = 0 if which == 0 else 2
    has_pos = pos is not None
    final_norm = final_gain is not None

    in_specs = [pl.BlockSpec((tm, d), lambda i, j: (i, 0))]
    args = [z]
    if has_pos:
        n_pos = pos.shape[0] // tm
        in_specs.append(pl.BlockSpec((tm, d), lambda i, j: (i % n_pos, 0)))
        args.append(pos)
    for k in range(3):
        in_specs.append(_mod_spec(layer, base + k, row_of_tile, d))
        args.append(mod)
    in_specs.append(_gain_spec(layer, gain_k, d))
    args.append(gains)
    if final_norm:
        in_specs.append(pl.BlockSpec((1, d), lambda i, j: (0, 0)))
        args.append(final_gain)
    in_specs += [
        pl.BlockSpec((None, None, d, tf), lambda i, j: (layer, which, 0, j)),
        pl.BlockSpec((None, None, d, tf), lambda i, j: (layer, which, 0, j)),
        pl.BlockSpec((None, None, tf, d), lambda i, j: (layer, which, j, 0)),
    ]
    args += [wg, wu, wd]

    return pl.pallas_call(
        functools.partial(_ffn_kernel, has_pos=has_pos, final_norm=final_norm),
        out_shape=jax.ShapeDtypeStruct((t, d), F32),
        grid=(t // tm, fp // tf),
        in_specs=in_specs,
        out_specs=pl.BlockSpec((tm, d), lambda i, j: (i, 0)),
        scratch_shapes=([pltpu.VMEM((tm, d), F32)] if has_pos else [])
        + [pltpu.VMEM((tm, d), BF16), pltpu.VMEM((tm, d), F32)],
        compiler_params=_cparams(2),
        name=f"ffn_l{layer}_{which}",
    )(*args)


def _inproj_kernel(z_ref, shift_ref, scale_ref, g_ref, w_ref, wz_ref, o_ref, oz_ref, u_scr):
    j = pl.program_id(1)

    @pl.when(j == 0)
    def _():
        u = _adanorm(z_ref[...], g_ref[...], shift_ref[...], scale_ref[...]).astype(BF16)
        u_scr[...] = u
        oz_ref[...] = _dot(u, wz_ref[...])

    o_ref[...] = _dot(u_scr[...], w_ref[...]).astype(o_ref.dtype)


def _inproj(z, mod, gains, w, wz, *, layer, row_of_tile, tm, tn, col0, n_out, name):
    t, d = z.shape
    cb0 = col0 // tn
    return pl.pallas_call(
        _inproj_kernel,
        out_shape=(jax.ShapeDtypeStruct((t, n_out), BF16),
                   jax.ShapeDtypeStruct((t, wz.shape[1]), F32)),
        grid=(t // tm, n_out // tn),
        in_specs=[
            pl.BlockSpec((tm, d), lambda i, j: (i, 0)),
            _mod_spec(layer, 3, row_of_tile, d),
            _mod_spec(layer, 4, row_of_tile, d),
            _gain_spec(layer, 1, d),
            pl.BlockSpec((d, tn), lambda i, j: (0, cb0 + j)),
            pl.BlockSpec(wz.shape, lambda i, j: (0, 0)),
        ],
        out_specs=(pl.BlockSpec((tm, tn), lambda i, j: (i, j)),
                   pl.BlockSpec((tm, wz.shape[1]), lambda i, j: (i, 0))),
        scratch_shapes=[pltpu.VMEM((tm, d), BF16)],
        compiler_params=_cparams(2),
        name=name,
    )(z, mod, mod, gains, w, wz)


def _log_sigmoid(x):
    return jnp.minimum(x, 0.0) - jnp.log1p(jnp.exp(-jnp.abs(x)))


def _gla_chunk(q, k, v, g, st_ref, *, reverse, q_scale):
    c = g.shape[0]
    row = lax.broadcasted_iota(jnp.int32, (c, c), 0)
    col = lax.broadcasted_iota(jnp.int32, (c, c), 1)
    keep = (col >= row) if reverse else (col <= row)
    b = _dot_f32(keep.astype(F32), g)
    b_end = b[0:1] if reverse else b[c - 1:c]

    o = None
    if q is not None:
        q = q * q_scale
        st16 = st_ref[...].astype(BF16)
        o = _dot_nt((q * jnp.exp(b)).astype(BF16), st16)
        rid = lax.broadcasted_iota(jnp.int32, (c, 1), 0)
        scores = []
        for blk in range(c // GLA_SUB):
            lo, hi = blk * GLA_SUB, (blk + 1) * GLA_SUB
            if reverse:
                ref = b[hi:hi + 1] if hi < c else jnp.zeros_like(b_end)
                visible = rid >= lo
            else:
                ref = b[lo - 1:lo] if lo > 0 else jnp.zeros_like(b_end)
                visible = rid < hi
            qb = (q[lo:hi] * jnp.exp(b[lo:hi] - ref)).astype(BF16)
            kb = (k * jnp.exp(jnp.where(visible, ref - b, 0.0))).astype(BF16)
            scores.append(_dot_nt(qb, kb))
        a = jnp.concatenate(scores, axis=0)
        a = jnp.where(keep, a, 0.0).astype(BF16)
        o = o + _dot(a, v)

    kd = (k * jnp.exp(b_end - b)).astype(BF16)
    st_ref[...] = st_ref[...] * jnp.exp(b_end) + _dot_tn(v, kd)
    return o


def _gla_scan_kernel(*refs, with_out, tb, q_scale):
    it = iter(refs)
    if with_out:
        qf_ref, qb_ref = next(it), next(it)
    kf_ref, kb_ref, vf_ref, vb_ref, zf_ref, zb_ref = (next(it) for _ in range(6))
    wdf_ref, bdf_ref, wdb_ref, bdb_ref, s0f_ref, s0b_ref = (next(it) for _ in range(6))
    if with_out:
        of_ref, ob_ref = next(it), next(it)
    else:
        sf_out, sb_out = next(it), next(it)
    sf_scr, sb_scr = next(it), next(it)

    step = pl.program_id(2)

    @pl.when(step == 0)
    def _():
        sf_scr[...] = s0f_ref[...]
        sb_scr[...] = s0b_ref[...]

    inv_temp = 1.0 / GLA_GATE_TEMP
    gf = _log_sigmoid(_dot_f32(zf_ref[...], wdf_ref[...]) + bdf_ref[...]) * inv_temp
    gb = _log_sigmoid(_dot_f32(zb_ref[...], wdb_ref[...]) + bdb_ref[...]) * inv_temp

    n_chunks = tb // GLA_CHUNK
    for ci in range(n_chunks):
        sl = slice(ci * GLA_CHUNK, (ci + 1) * GLA_CHUNK)
        o = _gla_chunk(qf_ref[sl, :].astype(F32) if with_out else None,
                       kf_ref[sl, :].astype(F32), vf_ref[sl, :], gf[sl],
                       sf_scr, reverse=False, q_scale=q_scale)
        if with_out:
            of_ref[sl, :] = o
    for ci in reversed(range(n_chunks)):
        sl = slice(ci * GLA_CHUNK, (ci + 1) * GLA_CHUNK)
        o = _gla_chunk(qb_ref[sl, :].astype(F32) if with_out else None,
                       kb_ref[sl, :].astype(F32), vb_ref[sl, :], gb[sl],
                       sb_scr, reverse=True, q_scale=q_scale)
        if with_out:
            ob_ref[sl, :] = o

    if not with_out:
        @pl.when(step == pl.num_programs(2) - 1)
        def _():
            sf_out[...] = sf_scr[...]
            sb_out[...] = sb_scr[...]


def _gla_scan(p, zg, wdf, bdf, wdb, bdb, s0f, s0b, *, with_out, q_col, k_col, v_col,
              dk, dv, tb, name):
    bsz, length, _ = p.shape
    nh = GLA_HEADS
    n = length // tb
    zw = zg.shape[-1]
    kq0, kk0, kv0 = q_col // dk, k_col // dk, v_col // dv

    def fwd(c):
        return c

    def bwd(c):
        return n - 1 - c

    def tile(width, c0, order):
        return pl.BlockSpec((None, tb, width), lambda b, h, c: (b, order(c), c0 + h))

    in_specs, args = [], []
    if with_out:
        in_specs += [tile(dk, kq0, fwd), tile(dk, kq0, bwd)]
        args += [p, p]
    in_specs += [tile(dk, kk0, fwd), tile(dk, kk0, bwd),
                 tile(dv, kv0, fwd), tile(dv, kv0, bwd),
                 pl.BlockSpec((None, tb, zw), lambda b, h, c: (b, c, 0)),
                 pl.BlockSpec((None, tb, zw), lambda b, h, c: (b, n - 1 - c, 0)),
                 pl.BlockSpec((zw, dk), lambda b, h, c: (0, h)),
                 pl.BlockSpec((1, dk), lambda b, h, c: (0, h)),
                 pl.BlockSpec((zw, dk), lambda b, h, c: (0, h)),
                 pl.BlockSpec((1, dk), lambda b, h, c: (0, h)),
                 pl.BlockSpec((None, None, dv, dk), lambda b, h, c: (b, h, 0, 0)),
                 pl.BlockSpec((None, None, dv, dk), lambda b, h, c: (b, h, 0, 0))]
    args += [p, p, p, p, zg, zg, wdf, bdf, wdb, bdb, s0f, s0b]

    if with_out:
        out_shape = (jax.ShapeDtypeStruct((bsz, length, nh * dv), F32),) * 2
        out_specs = (pl.BlockSpec((None, tb, dv), lambda b, h, c: (b, c, h)),
                     pl.BlockSpec((None, tb, dv), lambda b, h, c: (b, n - 1 - c, h)))
    else:
        out_shape = (jax.ShapeDtypeStruct((bsz, nh, dv, dk), F32),) * 2
        out_specs = (pl.BlockSpec((None, None, dv, dk), lambda b, h, c: (b, h, 0, 0)),) * 2

    return pl.pallas_call(
        functools.partial(_gla_scan_kernel, with_out=with_out, tb=tb, q_scale=dk ** -0.5),
        out_shape=out_shape,
        grid=(bsz, nh, n),
        in_specs=in_specs,
        out_specs=out_specs,
        scratch_shapes=[pltpu.VMEM((dv, dk), F32), pltpu.VMEM((dv, dk), F32)],
        compiler_params=_cparams(3),
        name=name,
    )(*args)


def _gla_out_kernel(of_ref, ob_ref, r_ref, og_ref, w_ref, h_ref, gate_ref, o_ref, *, dv):
    o = of_ref[...] + ob_ref[...]
    heads = []
    for hh in range(GLA_HEADS):
        oh = o[:, hh * dv:(hh + 1) * dv]
        heads.append(oh * lax.rsqrt(jnp.mean(oh * oh, axis=-1, keepdims=True) + EPS))
    o = jnp.concatenate(heads, axis=-1) * og_ref[...]
    y = (o * _silu(r_ref[...].astype(F32))).astype(BF16)
    o_ref[...] = h_ref[...] + gate_ref[...] * _dot(y, w_ref[...])


def _gla_out(of, ob, p, og, w_out, h, mod, *, layer, row_of_tile, tm, dv):
    t, d = h.shape
    n_in = w_out.shape[0]
    return pl.pallas_call(
        functools.partial(_gla_out_kernel, dv=dv),
        out_shape=jax.ShapeDtypeStruct((t, d), F32),
        grid=(t // tm,),
        in_specs=[
            pl.BlockSpec((tm, n_in), lambda i: (i, 0)),
            pl.BlockSpec((tm, n_in), lambda i: (i, 0)),
            pl.BlockSpec((tm, n_in), lambda i: (i, 0)),
            pl.BlockSpec((1, n_in), lambda i: (0, 0)),
            pl.BlockSpec((n_in, d), lambda i: (0, 0)),
            pl.BlockSpec((tm, d), lambda i: (i, 0)),
            _mod_spec(layer, 5, row_of_tile, d),
        ],
        out_specs=pl.BlockSpec((tm, d), lambda i: (i, 0)),
        compiler_params=_cparams(1),
        name="gla_out",
    )(of, ob, p, og, w_out, h, mod)


def _pw1_kernel(z_ref, shift_ref, scale_ref, g_ref, wa_ref, wg_ref, ba_ref, bg_ref, o_ref, u_scr):
    j = pl.program_id(1)

    @pl.when(j == 0)
    def _():
        u_scr[...] = _adanorm(z_ref[...], g_ref[...], shift_ref[...], scale_ref[...]).astype(BF16)

    u = u_scr[...]
    a = _dot(u, wa_ref[...]) + ba_ref[...]
    gt = _dot(u, wg_ref[...]) + bg_ref[...]
    o_ref[...] = a * jax.nn.sigmoid(gt)


def _pw1(z, mod, gains, w, b, *, layer, row_of_tile, tm, tn):
    t, d = z.shape
    n_half = w.shape[1] // 2
    nb = n_half // tn
    return pl.pallas_call(
        _pw1_kernel,
        out_shape=jax.ShapeDtypeStruct((t, n_half), F32),
        grid=(t // tm, nb),
        in_specs=[
            pl.BlockSpec((tm, d), lambda i, j: (i, 0)),
            _mod_spec(layer, 3, row_of_tile, d),
            _mod_spec(layer, 4, row_of_tile, d),
            _gain_spec(layer, 1, d),
            pl.BlockSpec((d, tn), lambda i, j: (0, j)),
            pl.BlockSpec((d, tn), lambda i, j: (0, nb + j)),
            pl.BlockSpec((1, tn), lambda i, j: (0, j)),
            pl.BlockSpec((1, tn), lambda i, j: (0, nb + j)),
        ],
        out_specs=pl.BlockSpec((tm, tn), lambda i, j: (i, j)),
        scratch_shapes=[pltpu.VMEM((tm, d), BF16)],
        compiler_params=_cparams(2),
        name="conv_pw1_glu",
    )(z, mod, mod, gains, w, w, b, b)


def _conv_kernel(zp_ref, zc_ref, zn_ref, wdw_ref, bdw_ref, lng_ref, lnb_ref, w2_ref, b2_ref,
                 h_ref, gate_ref, o_ref, ext_scr, dw_scr, *, width, rows_per_pass):
    i = pl.program_id(1)
    tt, d = zc_ref.shape
    halo = CONV_ROW_HALO
    pad = width // 2

    ext_scr[0:halo, :] = jnp.where(i > 0, zp_ref[...], 0.0)
    ext_scr[halo:halo + tt, :] = zc_ref[...]
    ext_scr[halo + tt:, :] = jnp.where(i < pl.num_programs(1) - 1, zn_ref[...], 0.0)

    def strip(cs, carry):
        cols = pl.ds(pl.multiple_of(cs * LANE, LANE), LANE)
        for rb in range(tt // rows_per_pass):
            r0 = rb * rows_per_pass
            acc = jnp.zeros((rows_per_pass, LANE), F32)
            for w in range(width):
                src = r0 + halo - pad + w
                acc = acc + ext_scr[src:src + rows_per_pass, cols] * wdw_ref[w:w + 1, cols]
            dw_scr[r0:r0 + rows_per_pass, cols] = acc
        return carry

    lax.fori_loop(0, d // LANE, strip, 0)

    zc = dw_scr[...] + bdw_ref[...]
    mu = jnp.mean(zc, axis=-1, keepdims=True)
    zc = zc - mu
    var = jnp.mean(zc * zc, axis=-1, keepdims=True)
    zn = zc * lax.rsqrt(var + EPS) * lng_ref[...] + lnb_ref[...]
    y = _dot(_silu(zn).astype(BF16), w2_ref[...]) + b2_ref[...]
    o_ref[...] = h_ref[...] + gate_ref[...] * y


def _conv(z, wdw, bdw, lng, lnb, w2, b2, h, mod, *, layer, bsz, tt):
    t, d = h.shape
    length = t // bsz
    nt = length // tt
    width = wdw.shape[0]
    hb = tt // CONV_ROW_HALO
    n_halo = length // CONV_ROW_HALO
    z3 = z.reshape(bsz, length, d)
    vec = pl.BlockSpec((1, d), lambda b, i: (0, 0))
    return pl.pallas_call(
        functools.partial(_conv_kernel, width=width, rows_per_pass=64),
        out_shape=jax.ShapeDtypeStruct((bsz, length, d), F32),
        grid=(bsz, nt),
        in_specs=[
            pl.BlockSpec((None, CONV_ROW_HALO, d), lambda b, i: (b, jnp.maximum(i * hb - 1, 0), 0)),
            pl.BlockSpec((None, tt, d), lambda b, i: (b, i, 0)),
            pl.BlockSpec((None, CONV_ROW_HALO, d),
                         lambda b, i: (b, jnp.minimum((i + 1) * hb, n_halo - 1), 0)),
            pl.BlockSpec((width, d), lambda b, i: (0, 0)),
            vec, vec, vec,
            pl.BlockSpec((d, d), lambda b, i: (0, 0)),
            vec,
            pl.BlockSpec((None, tt, d), lambda b, i: (b, i, 0)),
            pl.BlockSpec((None, None, None, 1, d), lambda b, i: (layer, b, 5, 0, 0)),
        ],
        out_specs=pl.BlockSpec((None, tt, d), lambda b, i: (b, i, 0)),
        scratch_shapes=[pltpu.VMEM((tt + 2 * CONV_ROW_HALO, d), F32), pltpu.VMEM((tt, d), F32)],
        compiler_params=_cparams(2),
        name="conv_dw_ln_pw2",
    )(z3, z3, z3, wdw, bdw, lng, lnb, w2, b2, h.reshape(bsz, length, d), mod).reshape(t, d)


def _pos_embed_2d(n_tok, d, dtype):
    rows = n_tok // GRID_W
    rr, cc = jnp.meshgrid(jnp.arange(rows), jnp.arange(GRID_W), indexing='ij')
    rr = rr.reshape(-1).astype(F32)
    cc = cc.reshape(-1).astype(F32)
    quarter = d // 4
    omega = 1.0 / (10000.0 ** (jnp.arange(quarter, dtype=F32) / quarter))
    ar = rr[:, None] * omega[None]
    ac = cc[:, None] * omega[None]
    return jnp.concatenate([jnp.sin(ar), jnp.cos(ar), jnp.sin(ac), jnp.cos(ac)], axis=-1).astype(dtype)


def _pick_tile(n, pref):
    t = min(pref, n)
    while n % t:
        t //= 2
    return t


def kernel(x, c, ctx, c_ctx, ada_w, ada_b, norm_g, final_norm_g, ffn_w_gate, ffn_w_up, ffn_w_down,
           gla_w_in, gla_w_decay_f, gla_b_decay_f, gla_w_decay_b, gla_b_decay_b, gla_out_norm_g,
           gla_w_out, conv_w_pw1, conv_b_pw1, conv_w_dw, conv_b_dw, conv_ln_g, conv_ln_b,
           conv_w_pw2, conv_b_pw2):
    bsz, length, d = x.shape
    ctx_len = ctx.shape[1]
    depth = ada_w.shape[0]
    assert depth == 2 and bsz + 1 <= MOD_ROWS
    f = ffn_w_gate.shape[-1]
    dk_all, dv_all = d // 2, d
    dk, dv = dk_all // GLA_HEADS, dv_all // GLA_HEADS
    rank = gla_w_decay_f.shape[1]
    t_lat, t_ctx = bsz * length, bsz * ctx_len

    tf = 512
    fp = -(-f // tf) * tf
    wg = jnp.pad(ffn_w_gate.astype(BF16), ((0, 0), (0, 0), (0, 0), (0, fp - f)))
    wu = jnp.pad(ffn_w_up.astype(BF16), ((0, 0), (0, 0), (0, 0), (0, fp - f)))
    wd = jnp.pad(ffn_w_down.astype(BF16), ((0, 0), (0, 0), (0, fp - f), (0, 0)))
    gains = norm_g.reshape(depth, 3, 1, d)

    w_in = gla_w_in[0]
    qr = dk_all + dv_all
    w_main = jnp.concatenate([w_in[:, dk_all:qr], w_in[:, :dk_all], w_in[:, qr:qr + dk_all + dv_all]],
                             axis=1).astype(BF16)
    n_main = w_main.shape[1]
    wz = jnp.pad(w_in[:, n_main:], ((0, 0), (0, LANE - 2 * rank))).astype(BF16)
    wdf = jnp.pad(gla_w_decay_f[0], ((0, LANE - rank), (0, 0)))
    wdb = jnp.pad(gla_w_decay_b[0], ((rank, LANE - 2 * rank), (0, 0)))
    bdf = gla_b_decay_f[0].reshape(1, dk_all)
    bdb = gla_b_decay_b[0].reshape(1, dk_all)
    og = gla_out_norm_g[0].reshape(1, dv_all)
    w_out = gla_w_out[0].astype(BF16)

    w_pw1 = conv_w_pw1[0].astype(BF16)
    b_pw1 = conv_b_pw1[0].reshape(1, -1)
    w_pw2 = conv_w_pw2[0].astype(BF16)

    cond = jnp.concatenate([c, c_ctx[None], jnp.zeros((MOD_ROWS - bsz - 1, d), F32)], axis=0)
    mod = _modulation(cond, ada_w, ada_b).reshape(depth, MOD_ROWS, N_MOD, 1, d)

    tm = _pick_tile(length, 512)
    tiles_per_seq = length // tm
    lat_row = lambda i: i // tiles_per_seq
    ctx_row = lambda i: bsz
    tm_ctx = _pick_tile(t_ctx, 512)

    pos = _pos_embed_2d(length, d, x.dtype)
    xf = x.reshape(t_lat, d)
    cf = ctx.reshape(t_ctx, d)

    h = _ffn(xf, mod, gains, wg, wu, wd, layer=0, which=0, row_of_tile=lat_row, tm=tm, tf=tf, pos=pos)
    hc = _ffn(cf, mod, gains, wg, wu, wd, layer=0, which=0, row_of_tile=ctx_row, tm=tm_ctx, tf=tf)

    tn = _pick_tile(dk_all, 512)
    p, zg = _inproj(h, mod, gains, w_main, wz, layer=0, row_of_tile=lat_row, tm=tm, tn=tn,
                    col0=0, n_out=n_main, name="gla_inproj")
    kv0 = dv_all + dk_all
    pc, zgc = _inproj(hc, mod, gains, w_main, wz, layer=0, row_of_tile=ctx_row, tm=tm_ctx, tn=tn,
                      col0=kv0, n_out=n_main - kv0, name="gla_inproj_ctx")

    zeros = jnp.zeros((bsz, GLA_HEADS, dv, dk), F32)
    tb_ctx = _pick_tile(ctx_len, 256)
    sf, sb = _gla_scan(pc.reshape(bsz, ctx_len, -1), zgc.reshape(bsz, ctx_len, -1),
                       wdf, bdf, wdb, bdb, zeros, zeros, with_out=False,
                       q_col=0, k_col=0, v_col=dk_all, dk=dk, dv=dv, tb=tb_ctx, name="gla_scan_ctx")
    tb = _pick_tile(length, 256)
    of, ob = _gla_scan(p.reshape(bsz, length, -1), zg.reshape(bsz, length, -1),
                       wdf, bdf, wdb, bdb, sf, sb, with_out=True,
                       q_col=dv_all, k_col=dv_all + dk_all, v_col=dv_all + 2 * dk_all,
                       dk=dk, dv=dv, tb=tb, name="gla_scan")
    tm_o = _pick_tile(length, 256)
    h = _gla_out(of.reshape(t_lat, dv_all), ob.reshape(t_lat, dv_all), p, og, w_out, h, mod,
                 layer=0, row_of_tile=lambda i: i // (length // tm_o), tm=tm_o, dv=dv)
    h = _ffn(h, mod, gains, wg, wu, wd, layer=0, which=1, row_of_tile=lat_row, tm=tm, tf=tf)

    h = _ffn(h, mod, gains, wg, wu, wd, layer=1, which=0, row_of_tile=lat_row, tm=tm, tf=tf)
    z = _pw1(h, mod, gains, w_pw1, b_pw1, layer=1, row_of_tile=lat_row, tm=tm, tn=tn)
    h = _conv(z, conv_w_dw[0], conv_b_dw[0].reshape(1, d), conv_ln_g[0].reshape(1, d),
              conv_ln_b[0].reshape(1, d), w_pw2, conv_b_pw2[0].reshape(1, d), h, mod,
              layer=1, bsz=bsz, tt=_pick_tile(length, 256))
    h = _ffn(h, mod, gains, wg, wu, wd, layer=1, which=1, row_of_tile=lat_row, tm=tm, tf=tf,
             final_gain=final_norm_g.reshape(1, d))
    return h.reshape(bsz, length, d)
```

```python
import functools

import jax
import jax.numpy as jnp
from jax import lax
from jax.experimental import pallas as pl
from jax.experimental.pallas import tpu as pltpu

F32 = jnp.float32
BF16 = jnp.bfloat16

EPS = 1e-6
N_MOD = 9
GRID_W = 64
GLA_HEADS = 4
GLA_GATE_TEMP = 16.0
GLA_BLOCK = 256
GLA_CHUNK = 64
GLA_SUB = 16
GATE_COPY = 32
CONV_ROW_HALO = 16
MOD_ROWS = 8
V7X_VMEM_LIMIT = 56 * 1024 * 1024
LANE = 128


def _cparams(n_axes):
    return pltpu.CompilerParams(
        dimension_semantics=("arbitrary",) * n_axes,
        vmem_limit_bytes=V7X_VMEM_LIMIT)


def _dot(a, b):
    return jnp.dot(a, b, preferred_element_type=F32)


def _dot_nt(a, b):
    return lax.dot_general(a, b, (((1,), (1,)), ((), ())), preferred_element_type=F32)


def _dot_tn(a, b):
    return lax.dot_general(a, b, (((0,), (0,)), ((), ())), preferred_element_type=F32)


def _silu(x):
    return x * jax.nn.sigmoid(x)


def _adanorm(z, g, shift, scale):
    y = z * lax.rsqrt(jnp.mean(z * z, axis=-1, keepdims=True) + EPS)
    return (y * g) * (1.0 + scale) + shift


def _mod_kernel(c_ref, w_ref, b_ref, o_ref):
    s = _silu(c_ref[...]).astype(BF16)
    o_ref[...] = _dot(s, w_ref[...].astype(BF16)) + b_ref[...]


def _modulation(cond, ada_w, ada_b):
    depth, d, n = ada_w.shape
    tn = _pick_tile(n, 1024)
    return pl.pallas_call(
        _mod_kernel,
        out_shape=jax.ShapeDtypeStruct((depth, MOD_ROWS, n), F32),
        grid=(depth, n // tn),
        in_specs=[
            pl.BlockSpec((MOD_ROWS, d), lambda l, j: (0, 0)),
            pl.BlockSpec((None, d, tn), lambda l, j: (l, 0, j)),
            pl.BlockSpec((None, 1, tn), lambda l, j: (l, 0, j)),
        ],
        out_specs=pl.BlockSpec((None, MOD_ROWS, tn), lambda l, j: (l, 0, j)),
        compiler_params=_cparams(2),
        name="ada_modulation",
    )(cond, ada_w, ada_b.reshape(depth, 1, n))


def _mod_spec(layer, k, row_of_tile, d):
    return pl.BlockSpec((None, None, None, 1, d),
                        lambda i, *_: (layer, row_of_tile(i), k, 0, 0))


def _gain_spec(layer, k, d):
    return pl.BlockSpec((None, None, 1, d), lambda *_: (layer, k, 0, 0))


def _ffn_kernel(*refs, has_pos, final_norm):
    it = iter(refs)
    z_ref = next(it)
    pos_ref = next(it) if has_pos else None
    shift_ref, scale_ref, gate_ref, g_ref = next(it), next(it), next(it), next(it)
    fg_ref = next(it) if final_norm else None
    wg_ref, wu_ref, wd_ref = next(it), next(it), next(it)
    o_ref = next(it)
    zres_ref = next(it) if has_pos else None
    u_scr, acc_scr = next(it), next(it)

    j = pl.program_id(1)

    @pl.when(j == 0)
    def _():
        z = z_ref[...]
        if has_pos:
            z = z + pos_ref[...]
            zres_ref[...] = z
        u = _adanorm(z, g_ref[...], shift_ref[...], scale_ref[...])
        u_scr[...] = u.astype(BF16)
        acc_scr[...] = jnp.zeros_like(acc_scr)

    u = u_scr[...]
    a = _dot(u, wg_ref[...])
    b = _dot(u, wu_ref[...])
    hid = (_silu(a) * b).astype(BF16)
    acc_scr[...] += _dot(hid, wd_ref[...])

    @pl.when(j == pl.num_programs(1) - 1)
    def _():
        z = zres_ref[...] if has_pos else z_ref[...]
        out = z + (0.5 * gate_ref[...]) * acc_scr[...]
        if final_norm:
            out = out * lax.rsqrt(jnp.mean(out * out, axis=-1, keepdims=True) + EPS) * fg_ref[...]
        o_ref[...] = out


def _ffn(z, mod, gains, wg, wu, wd, *, layer, which, row_of_tile, tm, tf,
         pos=None, final_gain=None):
    t, d = z.shape
    fp = wg.shape[-1]
    base = 0 if which == 0 else 6
    gain_k = 0 if which == 0 else 2
    has_pos = pos is not None
    final_norm = final_gain is not None

    in_specs = [pl.BlockSpec((tm, d), lambda i, j: (i, 0))]
    args = [z]
    if has_pos:
        n_pos = pos.shape[0] // tm
        in_specs.append(pl.BlockSpec((tm, d), lambda i, j: (i % n_pos, 0)))
        args.append(pos)
    for k in range(3):
        in_specs.append(_mod_spec(layer, base + k, row_of_tile, d))
        args.append(mod)
    in_specs.append(_gain_spec(layer, gain_k, d))
    args.append(gains)
    if final_norm:
        in_specs.append(pl.BlockSpec((1, d), lambda i, j: (0, 0)))
        args.append(final_gain)
    in_specs += [
        pl.BlockSpec((None, None, d, tf), lambda i, j: (layer, which, 0, j)),
        pl.BlockSpec((None, None, d, tf), lambda i, j: (layer, which, 0, j)),
        pl.BlockSpec((None, None, tf, d), lambda i, j: (layer, which, j, 0)),
    ]
    args += [wg, wu, wd]

    return pl.pallas_call(
        functools.partial(_ffn_kernel, has_pos=has_pos, final_norm=final_norm),
        out_shape=jax.ShapeDtypeStruct((t, d), F32),
        grid=(t // tm, fp // tf),
        in_specs=in_specs,
        out_specs=pl.BlockSpec((tm, d), lambda i, j: (i, 0)),
        scratch_shapes=([pltpu.VMEM((tm, d), F32)] if has_pos else [])
        + [pltpu.VMEM((tm, d), BF16), pltpu.VMEM((tm, d), F32)],
        compiler_params=_cparams(2),
        name=f"ffn_l{layer}_{which}",
    )(*args)


def _inproj_kernel(z_ref, shift_ref, scale_ref, g_ref, w_ref, wz_ref, o_ref, oz_ref, u_scr):
    j = pl.program_id(1)

    @pl.when(j == 0)
    def _():
        u = _adanorm(z_ref[...], g_ref[...], shift_ref[...], scale_ref[...]).astype(BF16)
        u_scr[...] = u
        oz_ref[...] = _dot(u, wz_ref[...])

    o_ref[...] = _dot(u_scr[...], w_ref[...]).astype(o_ref.dtype)


def _inproj(z, mod, gains, w, wz, *, layer, row_of_tile, tm, tn, col0, n_out, name):
    t, d = z.shape
    cb0 = col0 // tn
    return pl.pallas_call(
        _inproj_kernel,
        out_shape=(jax.ShapeDtypeStruct((t, n_out), BF16),
                   jax.ShapeDtypeStruct((t, wz.shape[1]), F32)),
        grid=(t // tm, n_out // tn),
        in_specs=[
            pl.BlockSpec((tm, d), lambda i, j: (i, 0)),
            _mod_spec(layer, 3, row_of_tile, d),
            _mod_spec(layer, 4, row_of_tile, d),
            _gain_spec(layer, 1, d),
            pl.BlockSpec((d, tn), lambda i, j: (0, cb0 + j)),
            pl.BlockSpec(wz.shape, lambda i, j: (0, 0)),
        ],
        out_specs=(pl.BlockSpec((tm, tn), lambda i, j: (i, j)),
                   pl.BlockSpec((tm, wz.shape[1]), lambda i, j: (i, 0))),
        scratch_shapes=[pltpu.VMEM((tm, d), BF16)],
        compiler_params=_cparams(2),
        name=name,
    )(z, mod, mod, gains, w, wz)


def _log_sigmoid(x):
    return jnp.minimum(x, 0.0) - jnp.log1p(jnp.exp(-jnp.abs(x)))


def _split_bf16(x, parts):
    out = []
    for _ in range(parts - 1):
        hi = x.astype(BF16)
        out.append(hi)
        x = x - hi.astype(F32)
    out.append(x.astype(BF16))
    return out


def _gate_log_decay(z, w, bias):
    lane = lax.broadcasted_iota(jnp.int32, z.shape, 1)
    is_lo = (lane >= GATE_COPY) & (lane < 2 * GATE_COPY)
    zcat = jnp.where(is_lo, z - z.astype(BF16).astype(F32), z).astype(BF16)
    return _log_sigmoid(_dot(zcat, w) + bias) * (1.0 / GLA_GATE_TEMP)


def _cumsum_rows(g, reverse):
    t = g.shape[0]
    row = lax.broadcasted_iota(jnp.int32, (t, t), 0)
    col = lax.broadcasted_iota(jnp.int32, (t, t), 1)
    tri = jnp.where((col >= row) if reverse else (col <= row), 1.0, 0.0).astype(BF16)
    g_hi, g_lo = _split_bf16(g, 2)
    return _dot(tri, g_hi) + _dot(tri, g_lo)


def _gla_block(q, k, v, b, st_ref, *, reverse):
    t = b.shape[0]
    n_sub, n_chunk, sub_per_chunk = t // GLA_SUB, t // GLA_CHUNK, GLA_CHUNK // GLA_SUB

    def rows(x, size, i):
        return x[i * size:(i + 1) * size]

    def far_edge(size, j):
        r = j * size if reverse else (j + 1) * size - 1
        return b[r:r + 1]

    b_end = far_edge(t, 0)
    e_chunk = [far_edge(GLA_CHUNK, j) for j in range(n_chunk)]
    k_chunk = jnp.concatenate(
        [rows(k, GLA_CHUNK, j) * jnp.exp(e_chunk[j] - rows(b, GLA_CHUNK, j)) for j in range(n_chunk)], axis=0)

    o = None
    if q is not None:
        o = _dot_nt((q * jnp.exp(b)).astype(BF16), st_ref[...].astype(BF16))
        e_sub = [far_edge(GLA_SUB, j) for j in range(n_sub)]
        k_sub = jnp.concatenate(
            [rows(k, GLA_SUB, j) * jnp.exp(e_sub[j] - rows(b, GLA_SUB, j)) for j in range(n_sub)], axis=0)

        def visible(i, n):
            return range(i, n) if reverse else range(i + 1)

        near = [(i, j) for i in range(n_sub) for j in visible(i, n_sub)
                if j // sub_per_chunk == i // sub_per_chunk]
        far = [(i, j) for i in range(n_chunk) for j in visible(i, n_chunk) if j != i]
        lhs_near = jnp.concatenate(
            [rows(q, GLA_SUB, i) * jnp.exp(rows(b, GLA_SUB, i) - e_sub[j]) for i, j in near], axis=0)
        lhs_far = jnp.concatenate(
            [rows(q, GLA_CHUNK, i) * jnp.exp(rows(b, GLA_CHUNK, i) - e_chunk[j]) for i, j in far], axis=0)
        r_near = _dot_nt(lhs_near.astype(BF16), k_sub.astype(BF16))
        r_far = _dot_nt(lhs_far.astype(BF16), k_chunk.astype(BF16))

        row = lax.broadcasted_iota(jnp.int32, (GLA_SUB, t), 0)
        col = lax.broadcasted_iota(jnp.int32, (GLA_SUB, t), 1)
        col_sub, col_chunk = col // GLA_SUB, col // GLA_CHUNK
        causal = (col - col_sub * GLA_SUB >= row) if reverse else (col - col_sub * GLA_SUB <= row)
        a_rows = []
        for i in range(n_sub):
            acc = jnp.zeros((GLA_SUB, t), F32)
            for n, (pi, pj) in enumerate(near):
                if pi == i:
                    sel = (col_sub == pj) & causal if pj == i else col_sub == pj
                    acc = jnp.where(sel, rows(r_near, GLA_SUB, n), acc)
            for n, (pi, pj) in enumerate(far):
                if pi == i // sub_per_chunk:
                    piece = rows(rows(r_far, GLA_CHUNK, n), GLA_SUB, i % sub_per_chunk)
                    acc = jnp.where(col_chunk == pj, piece, acc)
            a_rows.append(acc)
        o = o + _dot(jnp.concatenate(a_rows, axis=0).astype(BF16), v)

    kd = jnp.concatenate(
        [rows(k_chunk, GLA_CHUNK, j) * jnp.exp(b_end - e_chunk[j]) for j in range(n_chunk)], axis=0)
    st_ref[...] = st_ref[...] * jnp.exp(b_end) + _dot_tn(v, kd.astype(BF16))
    return o


def _gla_scan_kernel(*refs, with_out, q_scale):
    it = iter(refs)
    if with_out:
        qf_ref, qb_ref = next(it), next(it)
    kf_ref, kb_ref, vf_ref, vb_ref, zf_ref, zb_ref = (next(it) for _ in range(6))
    wdf_ref, bdf_ref, wdb_ref, bdb_ref, s0f_ref, s0b_ref = (next(it) for _ in range(6))
    if with_out:
        of_ref, ob_ref = next(it), next(it)
    else:
        sf_out, sb_out = next(it), next(it)
    sf_scr, sb_scr = next(it), next(it)

    step = pl.program_id(2)

    @pl.when(step == 0)
    def _():
        sf_scr[...] = s0f_ref[...]
        sb_scr[...] = s0b_ref[...]

    bf = _cumsum_rows(_gate_log_decay(zf_ref[...], wdf_ref[...], bdf_ref[...]), False)
    bb = _cumsum_rows(_gate_log_decay(zb_ref[...], wdb_ref[...], bdb_ref[...]), True)

    o = _gla_block(qf_ref[...].astype(F32) * q_scale if with_out else None,
                   kf_ref[...].astype(F32), vf_ref[...], bf, sf_scr, reverse=False)
    if with_out:
        of_ref[...] = o
    o = _gla_block(qb_ref[...].astype(F32) * q_scale if with_out else None,
                   kb_ref[...].astype(F32), vb_ref[...], bb, sb_scr, reverse=True)
    if with_out:
        ob_ref[...] = o

    if not with_out:
        @pl.when(step == pl.num_programs(2) - 1)
        def _():
            sf_out[...] = sf_scr[...]
            sb_out[...] = sb_scr[...]


def _gla_scan(p, zg, wdf, bdf, wdb, bdb, s0f, s0b, *, with_out, q_col, k_col, v_col,
              dk, dv, tb, name):
    bsz, length, _ = p.shape
    nh = GLA_HEADS
    n = length // tb
    zw = zg.shape[-1]
    kq0, kk0, kv0 = q_col // dk, k_col // dk, v_col // dv

    def fwd(c):
        return c

    def bwd(c):
        return n - 1 - c

    def tile(width, c0, order):
        return pl.BlockSpec((None, tb, width), lambda b, h, c: (b, order(c), c0 + h))

    in_specs, args = [], []
    if with_out:
        in_specs += [tile(dk, kq0, fwd), tile(dk, kq0, bwd)]
        args += [p, p]
    in_specs += [tile(dk, kk0, fwd), tile(dk, kk0, bwd),
                 tile(dv, kv0, fwd), tile(dv, kv0, bwd),
                 pl.BlockSpec((None, tb, zw), lambda b, h, c: (b, c, 0)),
                 pl.BlockSpec((None, tb, zw), lambda b, h, c: (b, n - 1 - c, 0)),
                 pl.BlockSpec((zw, dk), lambda b, h, c: (0, h)),
                 pl.BlockSpec((1, dk), lambda b, h, c: (0, h)),
                 pl.BlockSpec((zw, dk), lambda b, h, c: (0, h)),
                 pl.BlockSpec((1, dk), lambda b, h, c: (0, h)),
                 pl.BlockSpec((None, None, dv, dk), lambda b, h, c: (b, h, 0, 0)),
                 pl.BlockSpec((None, None, dv, dk), lambda b, h, c: (b, h, 0, 0))]
    args += [p, p, p, p, zg, zg, wdf, bdf, wdb, bdb, s0f, s0b]

    if with_out:
        out_shape = (jax.ShapeDtypeStruct((bsz, length, nh * dv), F32),) * 2
        out_specs = (pl.BlockSpec((None, tb, dv), lambda b, h, c: (b, c, h)),
                     pl.BlockSpec((None, tb, dv), lambda b, h, c: (b, n - 1 - c, h)))
    else:
        out_shape = (jax.ShapeDtypeStruct((bsz, nh, dv, dk), F32),) * 2
        out_specs = (pl.BlockSpec((None, None, dv, dk), lambda b, h, c: (b, h, 0, 0)),) * 2

    return pl.pallas_call(
        functools.partial(_gla_scan_kernel, with_out=with_out, q_scale=dk ** -0.5),
        out_shape=out_shape,
        grid=(bsz, nh, n),
        in_specs=in_specs,
        out_specs=out_specs,
        scratch_shapes=[pltpu.VMEM((dv, dk), F32), pltpu.VMEM((dv, dk), F32)],
        compiler_params=_cparams(3),
        name=name,
    )(*args)


def _gla_out_kernel(of_ref, ob_ref, r_ref, og_ref, w_ref, h_ref, gate_ref, o_ref, *, dv):
    o = of_ref[...] + ob_ref[...]
    heads = []
    for hh in range(GLA_HEADS):
        oh = o[:, hh * dv:(hh + 1) * dv]
        heads.append(oh * lax.rsqrt(jnp.mean(oh * oh, axis=-1, keepdims=True) + EPS))
    o = jnp.concatenate(heads, axis=-1) * og_ref[...]
    y = (o * _silu(r_ref[...].astype(F32))).astype(BF16)
    o_ref[...] = h_ref[...] + gate_ref[...] * _dot(y, w_ref[...])


def _gla_out(of, ob, p, og, w_out, h, mod, *, layer, row_of_tile, tm, dv):
    t, d = h.shape
    n_in = w_out.shape[0]
    return pl.pallas_call(
        functools.partial(_gla_out_kernel, dv=dv),
        out_shape=jax.ShapeDtypeStruct((t, d), F32),
        grid=(t // tm,),
        in_specs=[
            pl.BlockSpec((tm, n_in), lambda i: (i, 0)),
            pl.BlockSpec((tm, n_in), lambda i: (i, 0)),
            pl.BlockSpec((tm, n_in), lambda i: (i, 0)),
            pl.BlockSpec((1, n_in), lambda i: (0, 0)),
            pl.BlockSpec((n_in, d), lambda i: (0, 0)),
            pl.BlockSpec((tm, d), lambda i: (i, 0)),
            _mod_spec(layer, 5, row_of_tile, d),
        ],
        out_specs=pl.BlockSpec((tm, d), lambda i: (i, 0)),
        compiler_params=_cparams(1),
        name="gla_out",
    )(of, ob, p, og, w_out, h, mod)


def _pw1_kernel(z_ref, shift_ref, scale_ref, g_ref, wa_ref, wg_ref, ba_ref, bg_ref, o_ref, u_scr):
    j = pl.program_id(1)

    @pl.when(j == 0)
    def _():
        u_scr[...] = _adanorm(z_ref[...], g_ref[...], shift_ref[...], scale_ref[...]).astype(BF16)

    u = u_scr[...]
    a = _dot(u, wa_ref[...]) + ba_ref[...]
    gt = _dot(u, wg_ref[...]) + bg_ref[...]
    o_ref[...] = a * jax.nn.sigmoid(gt)


def _pw1(z, mod, gains, w, b, *, layer, row_of_tile, tm, tn):
    t, d = z.shape
    n_half = w.shape[1] // 2
    nb = n_half // tn
    return pl.pallas_call(
        _pw1_kernel,
        out_shape=jax.ShapeDtypeStruct((t, n_half), F32),
        grid=(t // tm, nb),
        in_specs=[
            pl.BlockSpec((tm, d), lambda i, j: (i, 0)),
            _mod_spec(layer, 3, row_of_tile, d),
            _mod_spec(layer, 4, row_of_tile, d),
            _gain_spec(layer, 1, d),
            pl.BlockSpec((d, tn), lambda i, j: (0, j)),
            pl.BlockSpec((d, tn), lambda i, j: (0, nb + j)),
            pl.BlockSpec((1, tn), lambda i, j: (0, j)),
            pl.BlockSpec((1, tn), lambda i, j: (0, nb + j)),
        ],
        out_specs=pl.BlockSpec((tm, tn), lambda i, j: (i, j)),
        scratch_shapes=[pltpu.VMEM((tm, d), BF16)],
        compiler_params=_cparams(2),
        name="conv_pw1_glu",
    )(z, mod, mod, gains, w, w, b, b)


def _conv_kernel(zp_ref, zc_ref, zn_ref, wdw_ref, bdw_ref, lng_ref, lnb_ref, w2_ref, b2_ref,
                 h_ref, gate_ref, o_ref, ext_scr, dw_scr, *, width, rows_per_pass):
    i = pl.program_id(1)
    tt, d = zc_ref.shape
    halo = CONV_ROW_HALO
    pad = width // 2

    ext_scr[0:halo, :] = jnp.where(i > 0, zp_ref[...], 0.0)
    ext_scr[halo:halo + tt, :] = zc_ref[...]
    ext_scr[halo + tt:, :] = jnp.where(i < pl.num_programs(1) - 1, zn_ref[...], 0.0)

    def strip(cs, carry):
        cols = pl.ds(pl.multiple_of(cs * LANE, LANE), LANE)
        for rb in range(tt // rows_per_pass):
            r0 = rb * rows_per_pass
            acc = jnp.zeros((rows_per_pass, LANE), F32)
            for w in range(width):
                src = r0 + halo - pad + w
                acc = acc + ext_scr[src:src + rows_per_pass, cols] * wdw_ref[w:w + 1, cols]
            dw_scr[r0:r0 + rows_per_pass, cols] = acc
        return carry

    lax.fori_loop(0, d // LANE, strip, 0)

    zc = dw_scr[...] + bdw_ref[...]
    mu = jnp.mean(zc, axis=-1, keepdims=True)
    zc = zc - mu
    var = jnp.mean(zc * zc, axis=-1, keepdims=True)
    zn = zc * lax.rsqrt(var + EPS) * lng_ref[...] + lnb_ref[...]
    y = _dot(_silu(zn).astype(BF16), w2_ref[...]) + b2_ref[...]
    o_ref[...] = h_ref[...] + gate_ref[...] * y


def _conv(z, wdw, bdw, lng, lnb, w2, b2, h, mod, *, layer, bsz, tt):
    t, d = h.shape
    length = t // bsz
    nt = length // tt
    width = wdw.shape[0]
    hb = tt // CONV_ROW_HALO
    n_halo = length // CONV_ROW_HALO
    z3 = z.reshape(bsz, length, d)
    vec = pl.BlockSpec((1, d), lambda b, i: (0, 0))
    return pl.pallas_call(
        functools.partial(_conv_kernel, width=width, rows_per_pass=64),
        out_shape=jax.ShapeDtypeStruct((bsz, length, d), F32),
        grid=(bsz, nt),
        in_specs=[
            pl.BlockSpec((None, CONV_ROW_HALO, d), lambda b, i: (b, jnp.maximum(i * hb - 1, 0), 0)),
            pl.BlockSpec((None, tt, d), lambda b, i: (b, i, 0)),
            pl.BlockSpec((None, CONV_ROW_HALO, d),
                         lambda b, i: (b, jnp.minimum((i + 1) * hb, n_halo - 1), 0)),
            pl.BlockSpec((width, d), lambda b, i: (0, 0)),
            vec, vec, vec,
            pl.BlockSpec((d, d), lambda b, i: (0, 0)),
            vec,
            pl.BlockSpec((None, tt, d), lambda b, i: (b, i, 0)),
            pl.BlockSpec((None, None, None, 1, d), lambda b, i: (layer, b, 5, 0, 0)),
        ],
        out_specs=pl.BlockSpec((None, tt, d), lambda b, i: (b, i, 0)),
        scratch_shapes=[pltpu.VMEM((tt + 2 * CONV_ROW_HALO, d), F32), pltpu.VMEM((tt, d), F32)],
        compiler_params=_cparams(2),
        name="conv_dw_ln_pw2",
    )(z3, z3, z3, wdw, bdw, lng, lnb, w2, b2, h.reshape(bsz, length, d), mod).reshape(t, d)


def _pos_embed_2d(n_tok, d, dtype):
    rows = n_tok // GRID_W
    rr, cc = jnp.meshgrid(jnp.arange(rows), jnp.arange(GRID_W), indexing='ij')
    rr = rr.reshape(-1).astype(F32)
    cc = cc.reshape(-1).astype(F32)
    quarter = d // 4
    omega = 1.0 / (10000.0 ** (jnp.arange(quarter, dtype=F32) / quarter))
    ar = rr[:, None] * omega[None]
    ac = cc[:, None] * omega[None]
    return jnp.concatenate([jnp.sin(ar), jnp.cos(ar), jnp.sin(ac), jnp.cos(ac)], axis=-1).astype(dtype)


def _pick_tile(n, pref):
    t = min(pref, n)
    while n % t:
        t //= 2
    return t


def kernel(x, c, ctx, c_ctx, ada_w, ada_b, norm_g, final_norm_g, ffn_w_gate, ffn_w_up, ffn_w_down,
           gla_w_in, gla_w_decay_f, gla_b_decay_f, gla_w_decay_b, gla_b_decay_b, gla_out_norm_g,
           gla_w_out, conv_w_pw1, conv_b_pw1, conv_w_dw, conv_b_dw, conv_ln_g, conv_ln_b,
           conv_w_pw2, conv_b_pw2):
    bsz, length, d = x.shape
    ctx_len = ctx.shape[1]
    depth = ada_w.shape[0]
    assert depth == 2 and bsz + 1 <= MOD_ROWS
    f = ffn_w_gate.shape[-1]
    dk_all, dv_all = d // 2, d
    dk, dv = dk_all // GLA_HEADS, dv_all // GLA_HEADS
    rank = gla_w_decay_f.shape[1]
    t_lat, t_ctx = bsz * length, bsz * ctx_len

    tf = 512
    fp = -(-f // tf) * tf
    wg = jnp.pad(ffn_w_gate.astype(BF16), ((0, 0), (0, 0), (0, 0), (0, fp - f)))
    wu = jnp.pad(ffn_w_up.astype(BF16), ((0, 0), (0, 0), (0, 0), (0, fp - f)))
    wd = jnp.pad(ffn_w_down.astype(BF16), ((0, 0), (0, 0), (0, fp - f), (0, 0)))
    gains = norm_g.reshape(depth, 3, 1, d)

    w_in = gla_w_in[0]
    qr = dk_all + dv_all
    w_main = jnp.concatenate([w_in[:, dk_all:qr], w_in[:, :dk_all], w_in[:, qr:qr + dk_all + dv_all]],
                             axis=1).astype(BF16)
    n_main = w_main.shape[1]
    assert 2 * rank == GATE_COPY
    wz1 = w_in[:, n_main:].astype(BF16)
    wz = jnp.concatenate([wz1, wz1, wz1, jnp.zeros((d, LANE - 3 * GATE_COPY), BF16)], axis=1)

    def gate_weight(w, first_row):
        hi = w.astype(BF16)
        lo = (w - hi.astype(F32)).astype(BF16)
        out = jnp.zeros((LANE, w.shape[1]), BF16)
        for copy, part in enumerate((hi, hi, lo)):
            out = lax.dynamic_update_slice(out, part, (copy * GATE_COPY + first_row, 0))
        return out

    wdf = gate_weight(gla_w_decay_f[0], 0)
    wdb = gate_weight(gla_w_decay_b[0], rank)
    bdf = gla_b_decay_f[0].reshape(1, dk_all)
    bdb = gla_b_decay_b[0].reshape(1, dk_all)
    og = gla_out_norm_g[0].reshape(1, dv_all)
    w_out = gla_w_out[0].astype(BF16)

    w_pw1 = conv_w_pw1[0].astype(BF16)
    b_pw1 = conv_b_pw1[0].reshape(1, -1)
    w_pw2 = conv_w_pw2[0].astype(BF16)

    cond = jnp.concatenate([c, c_ctx[None], jnp.zeros((MOD_ROWS - bsz - 1, d), F32)], axis=0)
    mod = _modulation(cond, ada_w, ada_b).reshape(depth, MOD_ROWS, N_MOD, 1, d)

    tm = _pick_tile(length, 512)
    tiles_per_seq = length // tm
    lat_row = lambda i: i // tiles_per_seq
    ctx_row = lambda i: bsz
    tm_ctx = _pick_tile(t_ctx, 512)

    pos = _pos_embed_2d(length, d, x.dtype)
    xf = x.reshape(t_lat, d)
    cf = ctx.reshape(t_ctx, d)

    h = _ffn(xf, mod, gains, wg, wu, wd, layer=0, which=0, row_of_tile=lat_row, tm=tm, tf=tf, pos=pos)
    hc = _ffn(cf, mod, gains, wg, wu, wd, layer=0, which=0, row_of_tile=ctx_row, tm=tm_ctx, tf=tf)

    tn = _pick_tile(dk_all, 512)
    p, zg = _inproj(h, mod, gains, w_main, wz, layer=0, row_of_tile=lat_row, tm=tm, tn=tn,
                    col0=0, n_out=n_main, name="gla_inproj")
    kv0 = dv_all + dk_all
    pc, zgc = _inproj(hc, mod, gains, w_main, wz, layer=0, row_of_tile=ctx_row, tm=tm_ctx, tn=tn,
                      col0=kv0, n_out=n_main - kv0, name="gla_inproj_ctx")

    zeros = jnp.zeros((bsz, GLA_HEADS, dv, dk), F32)
    tb_ctx = _pick_tile(ctx_len, GLA_BLOCK)
    sf, sb = _gla_scan(pc.reshape(bsz, ctx_len, -1), zgc.reshape(bsz, ctx_len, -1),
                       wdf, bdf, wdb, bdb, zeros, zeros, with_out=False,
                       q_col=0, k_col=0, v_col=dk_all, dk=dk, dv=dv, tb=tb_ctx, name="gla_scan_ctx")
    tb = _pick_tile(length, GLA_BLOCK)
    of, ob = _gla_scan(p.reshape(bsz, length, -1), zg.reshape(bsz, length, -1),
                       wdf, bdf, wdb, bdb, sf, sb, with_out=True,
                       q_col=dv_all, k_col=dv_all + dk_all, v_col=dv_all + 2 * dk_all,
                       dk=dk, dv=dv, tb=tb, name="gla_scan")
    tm_o = _pick_tile(length, 256)
    h = _gla_out(of.reshape(t_lat, dv_all), ob.reshape(t_lat, dv_all), p, og, w_out, h, mod,
                 layer=0, row_of_tile=lambda i: i // (length // tm_o), tm=tm_o, dv=dv)
    h = _ffn(h, mod, gains, wg, wu, wd, layer=0, which=1, row_of_tile=lat_row, tm=tm, tf=tf)

    h = _ffn(h, mod, gains, wg, wu, wd, layer=1, which=0, row_of_tile=lat_row, tm=tm, tf=tf)
    z = _pw1(h, mod, gains, w_pw1, b_pw1, layer=1, row_of_tile=lat_row, tm=tm, tn=tn)
    h = _conv(z, conv_w_dw[0], conv_b_dw[0].reshape(1, d), conv_ln_g[0].reshape(1, d),
              conv_ln_b[0].reshape(1, d), w_pw2, conv_b_pw2[0].reshape(1, d), h, mod,
              layer=1, bsz=bsz, tt=_pick_tile(length, 256))
    h = _ffn(h, mod, gains, wg, wu, wd, layer=1, which=1, row_of_tile=lat_row, tm=tm, tf=tf,
             final_gain=final_norm_g.reshape(1, d))
    return h.reshape(bsz, length, d)
```

```python
import functools

import jax
import jax.numpy as jnp
from jax import lax
from jax.experimental import pallas as pl
from jax.experimental.pallas import tpu as pltpu

F32 = jnp.float32
BF16 = jnp.bfloat16

EPS = 1e-6
N_MOD = 9
GRID_W = 64
GLA_HEADS = 4
GLA_GATE_TEMP = 16.0
GLA_BLOCK = 256
GLA_CHUNK = 64
GLA_SUB = 16
GATE_COPY = 32
FFN_SLICE = 256
SUBLANES = 8
CONV_ROW_HALO = 16
MOD_ROWS = 8
V7X_VMEM_LIMIT = 56 * 1024 * 1024
LANE = 128


def _cparams(n_axes):
    return pltpu.CompilerParams(
        dimension_semantics=("arbitrary",) * n_axes,
        vmem_limit_bytes=V7X_VMEM_LIMIT)


def _dot(a, b):
    return jnp.dot(a, b, preferred_element_type=F32)


def _dot_nt(a, b):
    return lax.dot_general(a, b, (((1,), (1,)), ((), ())), preferred_element_type=F32)


def _dot_tn(a, b):
    return lax.dot_general(a, b, (((0,), (0,)), ((), ())), preferred_element_type=F32)


def _silu(x):
    return x * jax.nn.sigmoid(x)


def _adanorm(z, g, shift, scale):
    y = z * lax.rsqrt(jnp.mean(z * z, axis=-1, keepdims=True) + EPS)
    return y * (g * (1.0 + scale)) + shift


def _mod_kernel(c_ref, w_ref, b_ref, o_ref):
    s = _silu(c_ref[...]).astype(BF16)
    o_ref[...] = _dot(s, w_ref[...].astype(BF16)) + b_ref[...]


def _modulation(cond, ada_w, ada_b):
    depth, d, n = ada_w.shape
    tn = _pick_tile(n, 1024)
    return pl.pallas_call(
        _mod_kernel,
        out_shape=jax.ShapeDtypeStruct((depth, MOD_ROWS, n), F32),
        grid=(depth, n // tn),
        in_specs=[
            pl.BlockSpec((MOD_ROWS, d), lambda l, j: (0, 0)),
            pl.BlockSpec((None, d, tn), lambda l, j: (l, 0, j)),
            pl.BlockSpec((None, 1, tn), lambda l, j: (l, 0, j)),
        ],
        out_specs=pl.BlockSpec((None, MOD_ROWS, tn), lambda l, j: (l, 0, j)),
        compiler_params=_cparams(2),
        name="ada_modulation",
    )(cond, ada_w, ada_b.reshape(depth, 1, n))


def _mod_spec(layer, k, row_of_tile, d):
    return pl.BlockSpec((None, None, None, 1, d),
                        lambda i, *_: (layer, row_of_tile(i), k, 0, 0))


def _gain_spec(layer, k, d):
    return pl.BlockSpec((None, None, 1, d), lambda *_: (layer, k, 0, 0))


def _ffn_kernel(*refs, has_pos, final_norm, has_tail):
    it = iter(refs)
    z_ref = next(it)
    pos_ref = next(it) if has_pos else None
    shift_ref, scale_ref, gate_ref, g_ref = next(it), next(it), next(it), next(it)
    fg_ref = next(it) if final_norm else None
    wg_ref, wu_ref, wd_ref = next(it), next(it), next(it)
    if has_tail:
        wg_tail, wu_tail, wd_tail = next(it), next(it), next(it)
    o_ref = next(it)
    u_scr = next(it)

    j = pl.program_id(1)

    def residual_in():
        return z_ref[...] + pos_ref[...] if has_pos else z_ref[...]

    def swiglu_down(u, wg, wu, wd):
        a = _dot(u, wg[...].astype(BF16))
        b = _dot(u, wu[...].astype(BF16))
        return _dot((_silu(a) * b).astype(BF16), wd[...].astype(BF16))

    @pl.when(j == 0)
    def _():
        u = _adanorm(residual_in(), g_ref[...], shift_ref[...], scale_ref[...]).astype(BF16)
        u_scr[...] = u
        o_ref[...] = swiglu_down(u, wg_tail, wu_tail, wd_tail) if has_tail else jnp.zeros_like(o_ref)

    o_ref[...] += swiglu_down(u_scr[...], wg_ref, wu_ref, wd_ref)

    @pl.when(j == pl.num_programs(1) - 1)
    def _():
        out = residual_in() + (0.5 * gate_ref[...]) * o_ref[...]
        if final_norm:
            out = out * lax.rsqrt(jnp.mean(out * out, axis=-1, keepdims=True) + EPS) * fg_ref[...]
        o_ref[...] = out


def _ffn(z, mod, gains, wg, wu, wd, *, layer, which, row_of_tile, tm, tf,
         pos=None, final_gain=None):
    t, d = z.shape
    f = wg.shape[-1]
    tail = f % tf
    n_main = f // tf
    has_tail = tail > 0
    assert tail % LANE == 0 and (not has_tail or (n_main * tf) % tail == 0)
    base = 0 if which == 0 else 6
    gain_k = 0 if which == 0 else 2
    has_pos = pos is not None
    final_norm = final_gain is not None
    once = pl.Buffered(1)

    in_specs = [pl.BlockSpec((tm, d), lambda i, j: (i, 0), pipeline_mode=once)]
    args = [z]
    if has_pos:
        n_pos = pos.shape[0] // tm
        in_specs.append(pl.BlockSpec((tm, d), lambda i, j: (i % n_pos, 0), pipeline_mode=once))
        args.append(pos)
    for k in range(3):
        in_specs.append(_mod_spec(layer, base + k, row_of_tile, d))
        args.append(mod)
    in_specs.append(_gain_spec(layer, gain_k, d))
    args.append(gains)
    if final_norm:
        in_specs.append(pl.BlockSpec((1, d), lambda i, j: (0, 0)))
        args.append(final_gain)
    in_specs += [
        pl.BlockSpec((None, None, d, tf), lambda i, j: (layer, which, 0, j)),
        pl.BlockSpec((None, None, d, tf), lambda i, j: (layer, which, 0, j)),
        pl.BlockSpec((None, None, tf, d), lambda i, j: (layer, which, j, 0)),
    ]
    args += [wg, wu, wd]
    if has_tail:
        tail_blk = n_main * tf // tail
        in_specs += [
            pl.BlockSpec((None, None, d, tail), lambda i, j: (layer, which, 0, tail_blk), pipeline_mode=once),
            pl.BlockSpec((None, None, d, tail), lambda i, j: (layer, which, 0, tail_blk), pipeline_mode=once),
            pl.BlockSpec((None, None, tail, d), lambda i, j: (layer, which, tail_blk, 0), pipeline_mode=once),
        ]
        args += [wg, wu, wd]

    return pl.pallas_call(
        functools.partial(_ffn_kernel, has_pos=has_pos, final_norm=final_norm, has_tail=has_tail),
        out_shape=jax.ShapeDtypeStruct((t, d), F32),
        grid=(t // tm, n_main),
        in_specs=in_specs,
        out_specs=pl.BlockSpec((tm, d), lambda i, j: (i, 0)),
        scratch_shapes=[pltpu.VMEM((tm, d), BF16)],
        compiler_params=_cparams(2),
        name=f"ffn_l{layer}_{which}",
    )(*args)


def _inproj_kernel(z_ref, shift_ref, scale_ref, g_ref, w_ref, wz_ref, o_ref, oz_ref, u_scr):
    j = pl.program_id(1)

    @pl.when(j == 0)
    def _():
        u = _adanorm(z_ref[...], g_ref[...], shift_ref[...], scale_ref[...]).astype(BF16)
        u_scr[...] = u
        oz_ref[...] = _dot(u, wz_ref[...])

    o_ref[...] = _dot(u_scr[...], w_ref[...]).astype(o_ref.dtype)


def _inproj(z, mod, gains, w, wz, *, layer, row_of_tile, tm, tn, col0, n_out, name):
    t, d = z.shape
    cb0 = col0 // tn
    return pl.pallas_call(
        _inproj_kernel,
        out_shape=(jax.ShapeDtypeStruct((t, n_out), BF16),
                   jax.ShapeDtypeStruct((t, wz.shape[1]), F32)),
        grid=(t // tm, n_out // tn),
        in_specs=[
            pl.BlockSpec((tm, d), lambda i, j: (i, 0)),
            _mod_spec(layer, 3, row_of_tile, d),
            _mod_spec(layer, 4, row_of_tile, d),
            _gain_spec(layer, 1, d),
            pl.BlockSpec((d, tn), lambda i, j: (0, cb0 + j)),
            pl.BlockSpec(wz.shape, lambda i, j: (0, 0)),
        ],
        out_specs=(pl.BlockSpec((tm, tn), lambda i, j: (i, j)),
                   pl.BlockSpec((tm, wz.shape[1]), lambda i, j: (i, 0))),
        scratch_shapes=[pltpu.VMEM((tm, d), BF16)],
        compiler_params=_cparams(2),
        name=name,
    )(z, mod, mod, gains, w, wz)


def _log_sigmoid(x):
    return jnp.minimum(x, 0.0) - jnp.log1p(jnp.exp(-jnp.abs(x)))


def _split_bf16(x, parts):
    out = []
    for _ in range(parts - 1):
        hi = x.astype(BF16)
        out.append(hi)
        x = x - hi.astype(F32)
    out.append(x.astype(BF16))
    return out


def _gate_log_decay(z, w, bias):
    lane = lax.broadcasted_iota(jnp.int32, z.shape, 1)
    is_lo = (lane >= GATE_COPY) & (lane < 2 * GATE_COPY)
    zcat = jnp.where(is_lo, z - z.astype(BF16).astype(F32), z).astype(BF16)
    return _log_sigmoid(_dot(zcat, w) + bias) * (1.0 / GLA_GATE_TEMP)


def _cumsum_rows(g, reverse):
    t = g.shape[0]
    row = lax.broadcasted_iota(jnp.int32, (t, t), 0)
    col = lax.broadcasted_iota(jnp.int32, (t, t), 1)
    tri = jnp.where((col >= row) if reverse else (col <= row), 1.0, 0.0).astype(BF16)
    g_hi, g_lo = _split_bf16(g, 2)
    return _dot(tri, g_hi) + _dot(tri, g_lo)


def _gla_block(q, k, v, b, st_ref, *, reverse):
    t = b.shape[0]
    n_sub, n_chunk, sub_per_chunk = t // GLA_SUB, t // GLA_CHUNK, GLA_CHUNK // GLA_SUB

    def rows(x, size, i):
        return x[i * size:(i + 1) * size]

    def far_edge(size, j):
        r = j * size if reverse else (j + 1) * size - 1
        return b[r:r + 1]

    b_end = far_edge(t, 0)
    e_chunk = [far_edge(GLA_CHUNK, j) for j in range(n_chunk)]
    k_chunk = jnp.concatenate(
        [rows(k, GLA_CHUNK, j) * jnp.exp(e_chunk[j] - rows(b, GLA_CHUNK, j)) for j in range(n_chunk)], axis=0)

    o = None
    if q is not None:
        o = _dot_nt((q * jnp.exp(b)).astype(BF16), st_ref[...].astype(BF16))
        e_sub = [far_edge(GLA_SUB, j) for j in range(n_sub)]
        k_sub = jnp.concatenate(
            [rows(k, GLA_SUB, j) * jnp.exp(e_sub[j] - rows(b, GLA_SUB, j)) for j in range(n_sub)], axis=0)

        def visible(i, n):
            return range(i, n) if reverse else range(i + 1)

        near = [(i, j) for i in range(n_sub) for j in visible(i, n_sub)
                if j // sub_per_chunk == i // sub_per_chunk]
        far = [(i, j) for i in range(n_chunk) for j in visible(i, n_chunk) if j != i]
        lhs_near = jnp.concatenate(
            [rows(q, GLA_SUB, i) * jnp.exp(rows(b, GLA_SUB, i) - e_sub[j]) for i, j in near], axis=0)
        lhs_far = jnp.concatenate(
            [rows(q, GLA_CHUNK, i) * jnp.exp(rows(b, GLA_CHUNK, i) - e_chunk[j]) for i, j in far], axis=0)
        r_near = _dot_nt(lhs_near.astype(BF16), k_sub.astype(BF16))
        r_far = _dot_nt(lhs_far.astype(BF16), k_chunk.astype(BF16))

        row = lax.broadcasted_iota(jnp.int32, (GLA_SUB, t), 0)
        col = lax.broadcasted_iota(jnp.int32, (GLA_SUB, t), 1)
        col_sub, col_chunk = col // GLA_SUB, col // GLA_CHUNK
        causal = (col - col_sub * GLA_SUB >= row) if reverse else (col - col_sub * GLA_SUB <= row)
        a_rows = []
        for i in range(n_sub):
            acc = jnp.zeros((GLA_SUB, t), F32)
            for n, (pi, pj) in enumerate(near):
                if pi == i:
                    sel = (col_sub == pj) & causal if pj == i else col_sub == pj
                    acc = jnp.where(sel, rows(r_near, GLA_SUB, n), acc)
            for n, (pi, pj) in enumerate(far):
                if pi == i // sub_per_chunk:
                    piece = rows(rows(r_far, GLA_CHUNK, n), GLA_SUB, i % sub_per_chunk)
                    acc = jnp.where(col_chunk == pj, piece, acc)
            a_rows.append(acc)
        o = o + _dot(jnp.concatenate(a_rows, axis=0).astype(BF16), v)

    kd = jnp.concatenate(
        [rows(k_chunk, GLA_CHUNK, j) * jnp.exp(b_end - e_chunk[j]) for j in range(n_chunk)], axis=0)
    st_ref[...] = st_ref[...] * jnp.exp(b_end) + _dot_tn(v, kd.astype(BF16))
    return o


def _gla_scan_kernel(*refs, with_out, q_scale):
    it = iter(refs)
    if with_out:
        qf_ref, qb_ref = next(it), next(it)
    kf_ref, kb_ref, vf_ref, vb_ref, zf_ref, zb_ref = (next(it) for _ in range(6))
    wdf_ref, bdf_ref, wdb_ref, bdb_ref, s0f_ref, s0b_ref = (next(it) for _ in range(6))
    if with_out:
        of_ref, ob_ref = next(it), next(it)
    else:
        sf_out, sb_out = next(it), next(it)
    sf_scr, sb_scr = next(it), next(it)

    step = pl.program_id(2)

    @pl.when(step == 0)
    def _():
        sf_scr[...] = s0f_ref[...]
        sb_scr[...] = s0b_ref[...]

    bf = _cumsum_rows(_gate_log_decay(zf_ref[...], wdf_ref[...], bdf_ref[...]), False)
    bb = _cumsum_rows(_gate_log_decay(zb_ref[...], wdb_ref[...], bdb_ref[...]), True)

    o = _gla_block(qf_ref[...].astype(F32) * q_scale if with_out else None,
                   kf_ref[...].astype(F32), vf_ref[...], bf, sf_scr, reverse=False)
    if with_out:
        of_ref[...] = o
    o = _gla_block(qb_ref[...].astype(F32) * q_scale if with_out else None,
                   kb_ref[...].astype(F32), vb_ref[...], bb, sb_scr, reverse=True)
    if with_out:
        ob_ref[...] = o

    if not with_out:
        @pl.when(step == pl.num_programs(2) - 1)
        def _():
            sf_out[...] = sf_scr[...]
            sb_out[...] = sb_scr[...]


def _gla_scan(p, zg, wdf, bdf, wdb, bdb, s0f, s0b, *, with_out, q_col, k_col, v_col,
              dk, dv, tb, name):
    bsz, length, _ = p.shape
    nh = GLA_HEADS
    n = length // tb
    zw = zg.shape[-1]
    kq0, kk0, kv0 = q_col // dk, k_col // dk, v_col // dv

    def fwd(c):
        return c

    def bwd(c):
        return n - 1 - c

    def tile(width, c0, order):
        return pl.BlockSpec((None, tb, width), lambda b, h, c: (b, order(c), c0 + h))

    in_specs, args = [], []
    if with_out:
        in_specs += [tile(dk, kq0, fwd), tile(dk, kq0, bwd)]
        args += [p, p]
    in_specs += [tile(dk, kk0, fwd), tile(dk, kk0, bwd),
                 tile(dv, kv0, fwd), tile(dv, kv0, bwd),
                 pl.BlockSpec((None, tb, zw), lambda b, h, c: (b, c, 0)),
                 pl.BlockSpec((None, tb, zw), lambda b, h, c: (b, n - 1 - c, 0)),
                 pl.BlockSpec((zw, dk), lambda b, h, c: (0, h)),
                 pl.BlockSpec((1, dk), lambda b, h, c: (0, h)),
                 pl.BlockSpec((zw, dk), lambda b, h, c: (0, h)),
                 pl.BlockSpec((1, dk), lambda b, h, c: (0, h)),
                 pl.BlockSpec((None, None, dv, dk), lambda b, h, c: (b, h, 0, 0)),
                 pl.BlockSpec((None, None, dv, dk), lambda b, h, c: (b, h, 0, 0))]
    args += [p, p, p, p, zg, zg, wdf, bdf, wdb, bdb, s0f, s0b]

    if with_out:
        out_shape = (jax.ShapeDtypeStruct((bsz, length, nh * dv), F32),) * 2
        out_specs = (pl.BlockSpec((None, tb, dv), lambda b, h, c: (b, c, h)),
                     pl.BlockSpec((None, tb, dv), lambda b, h, c: (b, n - 1 - c, h)))
    else:
        out_shape = (jax.ShapeDtypeStruct((bsz, nh, dv, dk), F32),) * 2
        out_specs = (pl.BlockSpec((None, None, dv, dk), lambda b, h, c: (b, h, 0, 0)),) * 2

    return pl.pallas_call(
        functools.partial(_gla_scan_kernel, with_out=with_out, q_scale=dk ** -0.5),
        out_shape=out_shape,
        grid=(bsz, nh, n),
        in_specs=in_specs,
        out_specs=out_specs,
        scratch_shapes=[pltpu.VMEM((dv, dk), F32), pltpu.VMEM((dv, dk), F32)],
        compiler_params=_cparams(3),
        name=name,
    )(*args)


def _gla_out_kernel(of_ref, ob_ref, r_ref, og_ref, w_ref, h_ref, gate_ref, o_ref, *, dv):
    o = of_ref[...] + ob_ref[...]
    heads = []
    for hh in range(GLA_HEADS):
        oh = o[:, hh * dv:(hh + 1) * dv]
        heads.append(oh * lax.rsqrt(jnp.mean(oh * oh, axis=-1, keepdims=True) + EPS))
    o = jnp.concatenate(heads, axis=-1) * og_ref[...]
    y = (o * _silu(r_ref[...].astype(F32))).astype(BF16)
    o_ref[...] = h_ref[...] + gate_ref[...] * _dot(y, w_ref[...])


def _gla_out(of, ob, p, og, w_out, h, mod, *, layer, row_of_tile, tm, dv):
    t, d = h.shape
    n_in = w_out.shape[0]
    return pl.pallas_call(
        functools.partial(_gla_out_kernel, dv=dv),
        out_shape=jax.ShapeDtypeStruct((t, d), F32),
        grid=(t // tm,),
        in_specs=[
            pl.BlockSpec((tm, n_in), lambda i: (i, 0)),
            pl.BlockSpec((tm, n_in), lambda i: (i, 0)),
            pl.BlockSpec((tm, n_in), lambda i: (i, 0)),
            pl.BlockSpec((1, n_in), lambda i: (0, 0)),
            pl.BlockSpec((n_in, d), lambda i: (0, 0), pipeline_mode=pl.Buffered(1)),
            pl.BlockSpec((tm, d), lambda i: (i, 0)),
            _mod_spec(layer, 5, row_of_tile, d),
        ],
        out_specs=pl.BlockSpec((tm, d), lambda i: (i, 0)),
        compiler_params=_cparams(1),
        name="gla_out",
    )(of, ob, p, og, w_out, h, mod)


def _pw1_kernel(z_ref, shift_ref, scale_ref, g_ref, wa_ref, wg_ref, ba_ref, bg_ref, o_ref, u_scr):
    j = pl.program_id(1)

    @pl.when(j == 0)
    def _():
        u_scr[...] = _adanorm(z_ref[...], g_ref[...], shift_ref[...], scale_ref[...]).astype(BF16)

    u = u_scr[...]
    a = _dot(u, wa_ref[...]) + ba_ref[...]
    gt = _dot(u, wg_ref[...]) + bg_ref[...]
    o_ref[...] = a * jax.nn.sigmoid(gt)


def _pw1(z, mod, gains, w, b, *, layer, row_of_tile, tm, tn):
    t, d = z.shape
    n_half = w.shape[1] // 2
    nb = n_half // tn
    return pl.pallas_call(
        _pw1_kernel,
        out_shape=jax.ShapeDtypeStruct((t, n_half), F32),
        grid=(t // tm, nb),
        in_specs=[
            pl.BlockSpec((tm, d), lambda i, j: (i, 0)),
            _mod_spec(layer, 3, row_of_tile, d),
            _mod_spec(layer, 4, row_of_tile, d),
            _gain_spec(layer, 1, d),
            pl.BlockSpec((d, tn), lambda i, j: (0, j)),
            pl.BlockSpec((d, tn), lambda i, j: (0, nb + j)),
            pl.BlockSpec((1, tn), lambda i, j: (0, j)),
            pl.BlockSpec((1, tn), lambda i, j: (0, nb + j)),
        ],
        out_specs=pl.BlockSpec((tm, tn), lambda i, j: (i, j)),
        scratch_shapes=[pltpu.VMEM((tm, d), BF16)],
        compiler_params=_cparams(2),
        name="conv_pw1_glu",
    )(z, mod, mod, gains, w, w, b, b)


def _conv_kernel(zp_ref, zc_ref, zn_ref, wdw_ref, bdw_ref, lng_ref, lnb_ref, w2_ref, b2_ref,
                 h_ref, gate_ref, o_ref, ext_scr, dw_scr, *, width, rows_per_pass):
    i = pl.program_id(1)
    tt, d = zc_ref.shape
    halo = CONV_ROW_HALO
    pad = width // 2

    ext_scr[0:halo, :] = jnp.where(i > 0, zp_ref[...], 0.0)
    ext_scr[halo:halo + tt, :] = zc_ref[...]
    ext_scr[halo + tt:, :] = jnp.where(i < pl.num_programs(1) - 1, zn_ref[...], 0.0)

    first = halo - pad

    def strip(cs, carry):
        cols = pl.ds(pl.multiple_of(cs * LANE, LANE), LANE)
        for rb in range(tt // rows_per_pass):
            r0 = rb * rows_per_pass
            acc = jnp.zeros((rows_per_pass, LANE), F32)
            for res in range(SUBLANES):
                part = None
                for s in range(first, first + width):
                    if s % SUBLANES != res:
                        continue
                    src = r0 + s - res
                    term = (ext_scr[src:src + rows_per_pass + SUBLANES, cols]
                            * wdw_ref[s - first:s - first + 1, cols])
                    part = term if part is None else part + term
                acc = acc + part[res:res + rows_per_pass]
            dw_scr[r0:r0 + rows_per_pass, cols] = acc
        return carry

    lax.fori_loop(0, d // LANE, strip, 0)

    zc = dw_scr[...] + bdw_ref[...]
    mu = jnp.mean(zc, axis=-1, keepdims=True)
    zc = zc - mu
    var = jnp.mean(zc * zc, axis=-1, keepdims=True)
    zn = zc * lax.rsqrt(var + EPS) * lng_ref[...] + lnb_ref[...]
    y = _dot(_silu(zn).astype(BF16), w2_ref[...]) + b2_ref[...]
    o_ref[...] = h_ref[...] + gate_ref[...] * y


def _conv(z, wdw, bdw, lng, lnb, w2, b2, h, mod, *, layer, bsz, tt):
    t, d = h.shape
    length = t // bsz
    nt = length // tt
    width = wdw.shape[0]
    hb = tt // CONV_ROW_HALO
    n_halo = length // CONV_ROW_HALO
    z3 = z.reshape(bsz, length, d)
    vec = pl.BlockSpec((1, d), lambda b, i: (0, 0))
    return pl.pallas_call(
        functools.partial(_conv_kernel, width=width, rows_per_pass=64),
        out_shape=jax.ShapeDtypeStruct((bsz, length, d), F32),
        grid=(bsz, nt),
        in_specs=[
            pl.BlockSpec((None, CONV_ROW_HALO, d), lambda b, i: (b, jnp.maximum(i * hb - 1, 0), 0)),
            pl.BlockSpec((None, tt, d), lambda b, i: (b, i, 0)),
            pl.BlockSpec((None, CONV_ROW_HALO, d),
                         lambda b, i: (b, jnp.minimum((i + 1) * hb, n_halo - 1), 0)),
            pl.BlockSpec((width, d), lambda b, i: (0, 0)),
            vec, vec, vec,
            pl.BlockSpec((d, d), lambda b, i: (0, 0), pipeline_mode=pl.Buffered(1)),
            vec,
            pl.BlockSpec((None, tt, d), lambda b, i: (b, i, 0)),
            pl.BlockSpec((None, None, None, 1, d), lambda b, i: (layer, b, 5, 0, 0)),
        ],
        out_specs=pl.BlockSpec((None, tt, d), lambda b, i: (b, i, 0)),
        scratch_shapes=[pltpu.VMEM((tt + 2 * CONV_ROW_HALO, d), F32), pltpu.VMEM((tt, d), F32)],
        compiler_params=_cparams(2),
        name="conv_dw_ln_pw2",
    )(z3, z3, z3, wdw, bdw, lng, lnb, w2, b2, h.reshape(bsz, length, d), mod).reshape(t, d)


def _pos_embed_2d(n_tok, d, dtype):
    rows = n_tok // GRID_W
    rr, cc = jnp.meshgrid(jnp.arange(rows), jnp.arange(GRID_W), indexing='ij')
    rr = rr.reshape(-1).astype(F32)
    cc = cc.reshape(-1).astype(F32)
    quarter = d // 4
    omega = 1.0 / (10000.0 ** (jnp.arange(quarter, dtype=F32) / quarter))
    ar = rr[:, None] * omega[None]
    ac = cc[:, None] * omega[None]
    return jnp.concatenate([jnp.sin(ar), jnp.cos(ar), jnp.sin(ac), jnp.cos(ac)], axis=-1).astype(dtype)


def _pick_tile(n, pref):
    t = min(pref, n)
    while n % t:
        t //= 2
    return t


def kernel(x, c, ctx, c_ctx, ada_w, ada_b, norm_g, final_norm_g, ffn_w_gate, ffn_w_up, ffn_w_down,
           gla_w_in, gla_w_decay_f, gla_b_decay_f, gla_w_decay_b, gla_b_decay_b, gla_out_norm_g,
           gla_w_out, conv_w_pw1, conv_b_pw1, conv_w_dw, conv_b_dw, conv_ln_g, conv_ln_b,
           conv_w_pw2, conv_b_pw2):
    bsz, length, d = x.shape
    ctx_len = ctx.shape[1]
    depth = ada_w.shape[0]
    assert depth == 2 and bsz + 1 <= MOD_ROWS
    f = ffn_w_gate.shape[-1]
    dk_all, dv_all = d // 2, d
    dk, dv = dk_all // GLA_HEADS, dv_all // GLA_HEADS
    rank = gla_w_decay_f.shape[1]
    t_lat, t_ctx = bsz * length, bsz * ctx_len

    gains = norm_g.reshape(depth, 3, 1, d)

    w_in = gla_w_in[0]
    qr = dk_all + dv_all
    w_main = jnp.concatenate([w_in[:, dk_all:qr], w_in[:, :dk_all], w_in[:, qr:qr + dk_all + dv_all]],
                             axis=1).astype(BF16)
    n_main = w_main.shape[1]
    assert 2 * rank == GATE_COPY
    wz1 = w_in[:, n_main:].astype(BF16)
    wz = jnp.concatenate([wz1, wz1, wz1, jnp.zeros((d, LANE - 3 * GATE_COPY), BF16)], axis=1)

    def gate_weight(w, first_row):
        hi = w.astype(BF16)
        lo = (w - hi.astype(F32)).astype(BF16)
        out = jnp.zeros((LANE, w.shape[1]), BF16)
        for copy, part in enumerate((hi, hi, lo)):
            out = lax.dynamic_update_slice(out, part, (copy * GATE_COPY + first_row, 0))
        return out

    wdf = gate_weight(gla_w_decay_f[0], 0)
    wdb = gate_weight(gla_w_decay_b[0], rank)
    bdf = gla_b_decay_f[0].reshape(1, dk_all)
    bdb = gla_b_decay_b[0].reshape(1, dk_all)
    og = gla_out_norm_g[0].reshape(1, dv_all)
    w_out = gla_w_out[0].astype(BF16)

    w_pw1 = conv_w_pw1[0].astype(BF16)
    b_pw1 = conv_b_pw1[0].reshape(1, -1)
    w_pw2 = conv_w_pw2[0].astype(BF16)

    cond = jnp.concatenate([c, c_ctx[None], jnp.zeros((MOD_ROWS - bsz - 1, d), F32)], axis=0)
    mod = _modulation(cond, ada_w, ada_b).reshape(depth, MOD_ROWS, N_MOD, 1, d)

    def lat_row(tm):
        return lambda i: i // (length // tm)

    ctx_row = lambda i: bsz

    def ffn(zz, layer, which, tm, row, **kw):
        return _ffn(zz, mod, gains, ffn_w_gate, ffn_w_up, ffn_w_down, layer=layer, which=which,
                    row_of_tile=row, tm=tm, tf=FFN_SLICE, **kw)

    tm = _pick_tile(length, 512)
    tm_ffn = _pick_tile(length, 1024)
    tm_ctx = _pick_tile(t_ctx, 1024)

    pos = _pos_embed_2d(length, d, x.dtype)
    xf = x.reshape(t_lat, d)
    cf = ctx.reshape(t_ctx, d)

    h = ffn(xf, 0, 0, tm, lat_row(tm), pos=pos)
    hc = ffn(cf, 0, 0, tm_ctx, ctx_row)

    p, zg = _inproj(h, mod, gains, w_main, wz, layer=0, row_of_tile=lat_row(tm), tm=tm,
                    tn=_pick_tile(2 * dk_all, 2048), col0=0, n_out=n_main, name="gla_inproj")
    kv0 = dv_all + dk_all
    pc, zgc = _inproj(hc, mod, gains, w_main, wz, layer=0, row_of_tile=ctx_row, tm=_pick_tile(t_ctx, 512),
                      tn=_pick_tile(dk_all, 1024), col0=kv0, n_out=n_main - kv0, name="gla_inproj_ctx")

    zeros = jnp.zeros((bsz, GLA_HEADS, dv, dk), F32)
    tb_ctx = _pick_tile(ctx_len, GLA_BLOCK)
    sf, sb = _gla_scan(pc.reshape(bsz, ctx_len, -1), zgc.reshape(bsz, ctx_len, -1),
                       wdf, bdf, wdb, bdb, zeros, zeros, with_out=False,
                       q_col=0, k_col=0, v_col=dk_all, dk=dk, dv=dv, tb=tb_ctx, name="gla_scan_ctx")
    tb = _pick_tile(length, GLA_BLOCK)
    of, ob = _gla_scan(p.reshape(bsz, length, -1), zg.reshape(bsz, length, -1),
                       wdf, bdf, wdb, bdb, sf, sb, with_out=True,
                       q_col=dv_all, k_col=dv_all + dk_all, v_col=dv_all + 2 * dk_all,
                       dk=dk, dv=dv, tb=tb, name="gla_scan")
    tm_o = _pick_tile(length, 256)
    h = _gla_out(of.reshape(t_lat, dv_all), ob.reshape(t_lat, dv_all), p, og, w_out, h, mod,
                 layer=0, row_of_tile=lat_row(tm_o), tm=tm_o, dv=dv)
    h = ffn(h, 0, 1, tm_ffn, lat_row(tm_ffn))

    h = ffn(h, 1, 0, tm_ffn, lat_row(tm_ffn))
    z = _pw1(h, mod, gains, w_pw1, b_pw1, layer=1, row_of_tile=lat_row(tm), tm=tm, tn=_pick_tile(d, 1024))
    h = _conv(z, conv_w_dw[0], conv_b_dw[0].reshape(1, d), conv_ln_g[0].reshape(1, d),
              conv_ln_b[0].reshape(1, d), w_pw2, conv_b_pw2[0].reshape(1, d), h, mod,
              layer=1, bsz=bsz, tt=_pick_tile(length, 256))
    h = ffn(h, 1, 1, tm_ffn, lat_row(tm_ffn), final_gain=final_norm_g.reshape(1, d))
    return h.reshape(bsz, length, d)
```

```python
import functools

import jax
import jax.numpy as jnp
from jax import lax
from jax.experimental import pallas as pl
from jax.experimental.pallas import tpu as pltpu

F32 = jnp.float32
BF16 = jnp.bfloat16

EPS = 1e-6
N_MOD = 9
GRID_W = 64
GLA_HEADS = 4
GLA_GATE_TEMP = 16.0
GLA_BLOCK = 256
GLA_CHUNK = 64
GLA_SUB = 16
GATE_COPY = 32
FFN_SLICE = 256
SUBLANES = 8
CONV_ROW_HALO = 16
MOD_ROWS = 8
V7X_VMEM_LIMIT = 56 * 1024 * 1024
LANE = 128


def _cparams(n_axes):
    return pltpu.CompilerParams(
        dimension_semantics=("arbitrary",) * n_axes,
        vmem_limit_bytes=V7X_VMEM_LIMIT)


def _dot(a, b):
    return jnp.dot(a, b, preferred_element_type=F32)


def _dot_nt(a, b):
    return lax.dot_general(a, b, (((1,), (1,)), ((), ())), preferred_element_type=F32)


def _dot_tn(a, b):
    return lax.dot_general(a, b, (((0,), (0,)), ((), ())), preferred_element_type=F32)


def _silu(x):
    return x * jax.nn.sigmoid(x)


def _adanorm(z, g, shift, scale):
    y = z * lax.rsqrt(jnp.mean(z * z, axis=-1, keepdims=True) + EPS)
    return y * (g * (1.0 + scale)) + shift


def _mod_kernel(c_ref, w_ref, b_ref, o_ref):
    s = _silu(c_ref[...]).astype(BF16)
    o_ref[...] = _dot(s, w_ref[...].astype(BF16)) + b_ref[...]


def _modulation(cond, ada_w, ada_b):
    depth, d, n = ada_w.shape
    tn = _pick_tile(n, 1024)
    return pl.pallas_call(
        _mod_kernel,
        out_shape=jax.ShapeDtypeStruct((depth, MOD_ROWS, n), F32),
        grid=(depth, n // tn),
        in_specs=[
            pl.BlockSpec((MOD_ROWS, d), lambda l, j: (0, 0)),
            pl.BlockSpec((None, d, tn), lambda l, j: (l, 0, j)),
            pl.BlockSpec((None, 1, tn), lambda l, j: (l, 0, j)),
        ],
        out_specs=pl.BlockSpec((None, MOD_ROWS, tn), lambda l, j: (l, 0, j)),
        compiler_params=_cparams(2),
        name="ada_modulation",
    )(cond, ada_w, ada_b.reshape(depth, 1, n))


def _mod_spec(layer, k, row_of_tile, d):
    return pl.BlockSpec((None, None, None, 1, d),
                        lambda i, *_: (layer, row_of_tile(i), k, 0, 0))


def _gain_spec(layer, k, d):
    return pl.BlockSpec((None, None, 1, d), lambda *_: (layer, k, 0, 0))


def _ffn_kernel(*refs, has_pos, final_norm, has_tail):
    it = iter(refs)
    z_ref = next(it)
    pos_ref = next(it) if has_pos else None
    shift_ref, scale_ref, gate_ref, g_ref = next(it), next(it), next(it), next(it)
    fg_ref = next(it) if final_norm else None
    wg_ref, wu_ref, wd_ref = next(it), next(it), next(it)
    if has_tail:
        wg_tail, wu_tail, wd_tail = next(it), next(it), next(it)
    o_ref = next(it)
    u_scr = next(it)

    j = pl.program_id(1)

    def residual_in():
        return z_ref[...] + pos_ref[...] if has_pos else z_ref[...]

    def swiglu_down(u, wg, wu, wd):
        a = _dot(u, wg[...])
        b = _dot(u, wu[...])
        return _dot((_silu(a) * b).astype(BF16), wd[...])

    @pl.when(j == 0)
    def _():
        u = _adanorm(residual_in(), g_ref[...], shift_ref[...], scale_ref[...]).astype(BF16)
        u_scr[...] = u
        o_ref[...] = swiglu_down(u, wg_tail, wu_tail, wd_tail) if has_tail else jnp.zeros_like(o_ref)

    o_ref[...] += swiglu_down(u_scr[...], wg_ref, wu_ref, wd_ref)

    @pl.when(j == pl.num_programs(1) - 1)
    def _():
        out = residual_in() + (0.5 * gate_ref[...]) * o_ref[...]
        if final_norm:
            out = out * lax.rsqrt(jnp.mean(out * out, axis=-1, keepdims=True) + EPS) * fg_ref[...]
        o_ref[...] = out


def _ffn(z, mod, gains, wg, wu, wd, *, layer, which, row_of_tile, tm, tf,
         pos=None, final_gain=None):
    t, d = z.shape
    f = wg.shape[-1]
    tail = f % tf
    n_main = f // tf
    has_tail = tail > 0
    assert tail % LANE == 0 and (not has_tail or (n_main * tf) % tail == 0)
    base = 0 if which == 0 else 6
    gain_k = 0 if which == 0 else 2
    has_pos = pos is not None
    final_norm = final_gain is not None
    once = pl.Buffered(1)

    in_specs = [pl.BlockSpec((tm, d), lambda i, j: (i, 0), pipeline_mode=once)]
    args = [z]
    if has_pos:
        n_pos = pos.shape[0] // tm
        in_specs.append(pl.BlockSpec((tm, d), lambda i, j: (i % n_pos, 0), pipeline_mode=once))
        args.append(pos)
    for k in range(3):
        in_specs.append(_mod_spec(layer, base + k, row_of_tile, d))
        args.append(mod)
    in_specs.append(_gain_spec(layer, gain_k, d))
    args.append(gains)
    if final_norm:
        in_specs.append(pl.BlockSpec((1, d), lambda i, j: (0, 0)))
        args.append(final_gain)
    in_specs += [
        pl.BlockSpec((None, None, d, tf), lambda i, j: (layer, which, 0, j)),
        pl.BlockSpec((None, None, d, tf), lambda i, j: (layer, which, 0, j)),
        pl.BlockSpec((None, None, tf, d), lambda i, j: (layer, which, j, 0)),
    ]
    args += [wg, wu, wd]
    if has_tail:
        tail_blk = n_main * tf // tail
        in_specs += [
            pl.BlockSpec((None, None, d, tail), lambda i, j: (layer, which, 0, tail_blk), pipeline_mode=once),
            pl.BlockSpec((None, None, d, tail), lambda i, j: (layer, which, 0, tail_blk), pipeline_mode=once),
            pl.BlockSpec((None, None, tail, d), lambda i, j: (layer, which, tail_blk, 0), pipeline_mode=once),
        ]
        args += [wg, wu, wd]

    return pl.pallas_call(
        functools.partial(_ffn_kernel, has_pos=has_pos, final_norm=final_norm, has_tail=has_tail),
        out_shape=jax.ShapeDtypeStruct((t, d), F32),
        grid=(t // tm, n_main),
        in_specs=in_specs,
        out_specs=pl.BlockSpec((tm, d), lambda i, j: (i, 0)),
        scratch_shapes=[pltpu.VMEM((tm, d), BF16)],
        compiler_params=_cparams(2),
        name=f"ffn_l{layer}_{which}",
    )(*args)


def _inproj_kernel(z_ref, shift_ref, scale_ref, g_ref, w_ref, wz_ref, o_ref, oz_ref, u_scr):
    j = pl.program_id(1)

    @pl.when(j == 0)
    def _():
        u = _adanorm(z_ref[...], g_ref[...], shift_ref[...], scale_ref[...]).astype(BF16)
        u_scr[...] = u
        oz_ref[...] = _dot(u, wz_ref[...])

    o_ref[...] = _dot(u_scr[...], w_ref[...]).astype(o_ref.dtype)


def _inproj(z, mod, gains, w, wz, *, layer, row_of_tile, tm, tn, col0, n_out, name):
    t, d = z.shape
    cb0 = col0 // tn
    return pl.pallas_call(
        _inproj_kernel,
        out_shape=(jax.ShapeDtypeStruct((t, n_out), BF16),
                   jax.ShapeDtypeStruct((t, wz.shape[1]), F32)),
        grid=(t // tm, n_out // tn),
        in_specs=[
            pl.BlockSpec((tm, d), lambda i, j: (i, 0)),
            _mod_spec(layer, 3, row_of_tile, d),
            _mod_spec(layer, 4, row_of_tile, d),
            _gain_spec(layer, 1, d),
            pl.BlockSpec((d, tn), lambda i, j: (0, cb0 + j)),
            pl.BlockSpec(wz.shape, lambda i, j: (0, 0)),
        ],
        out_specs=(pl.BlockSpec((tm, tn), lambda i, j: (i, j)),
                   pl.BlockSpec((tm, wz.shape[1]), lambda i, j: (i, 0))),
        scratch_shapes=[pltpu.VMEM((tm, d), BF16)],
        compiler_params=_cparams(2),
        name=name,
    )(z, mod, mod, gains, w, wz)


def _log_sigmoid(x):
    return jnp.minimum(x, 0.0) - jnp.log1p(jnp.exp(-jnp.abs(x)))


def _split_bf16(x, parts):
    out = []
    for _ in range(parts - 1):
        hi = x.astype(BF16)
        out.append(hi)
        x = x - hi.astype(F32)
    out.append(x.astype(BF16))
    return out


def _gate_log_decay(z, w, bias):
    lane = lax.broadcasted_iota(jnp.int32, z.shape, 1)
    is_lo = (lane >= GATE_COPY) & (lane < 2 * GATE_COPY)
    zcat = jnp.where(is_lo, z - z.astype(BF16).astype(F32), z).astype(BF16)
    return _log_sigmoid(_dot(zcat, w) + bias) * (1.0 / GLA_GATE_TEMP)


def _cumsum_rows(g, reverse):
    t = g.shape[0]
    row = lax.broadcasted_iota(jnp.int32, (t, t), 0)
    col = lax.broadcasted_iota(jnp.int32, (t, t), 1)
    tri = jnp.where((col >= row) if reverse else (col <= row), 1.0, 0.0).astype(BF16)
    g_hi, g_lo = _split_bf16(g, 2)
    return _dot(tri, g_hi) + _dot(tri, g_lo)


def _gla_block(q, k, v, b, st_ref, *, reverse):
    t = b.shape[0]
    n_sub, n_chunk, sub_per_chunk = t // GLA_SUB, t // GLA_CHUNK, GLA_CHUNK // GLA_SUB

    def rows(x, size, i):
        return x[i * size:(i + 1) * size]

    def far_edge(size, j):
        r = j * size if reverse else (j + 1) * size - 1
        return b[r:r + 1]

    b_end = far_edge(t, 0)
    e_chunk = [far_edge(GLA_CHUNK, j) for j in range(n_chunk)]
    k_chunk = jnp.concatenate(
        [rows(k, GLA_CHUNK, j) * jnp.exp(e_chunk[j] - rows(b, GLA_CHUNK, j)) for j in range(n_chunk)], axis=0)

    o = None
    if q is not None:
        o = _dot_nt((q * jnp.exp(b)).astype(BF16), st_ref[...].astype(BF16))
        e_sub = [far_edge(GLA_SUB, j) for j in range(n_sub)]
        k_sub = jnp.concatenate(
            [rows(k, GLA_SUB, j) * jnp.exp(e_sub[j] - rows(b, GLA_SUB, j)) for j in range(n_sub)], axis=0)

        def visible(i, n):
            return range(i, n) if reverse else range(i + 1)

        near = [(i, j) for i in range(n_sub) for j in visible(i, n_sub)
                if j // sub_per_chunk == i // sub_per_chunk]
        far = [(i, j) for i in range(n_chunk) for j in visible(i, n_chunk) if j != i]
        lhs_near = jnp.concatenate(
            [rows(q, GLA_SUB, i) * jnp.exp(rows(b, GLA_SUB, i) - e_sub[j]) for i, j in near], axis=0)
        lhs_far = jnp.concatenate(
            [rows(q, GLA_CHUNK, i) * jnp.exp(rows(b, GLA_CHUNK, i) - e_chunk[j]) for i, j in far], axis=0)
        r_near = _dot_nt(lhs_near.astype(BF16), k_sub.astype(BF16))
        r_far = _dot_nt(lhs_far.astype(BF16), k_chunk.astype(BF16))

        row = lax.broadcasted_iota(jnp.int32, (GLA_SUB, t), 0)
        col = lax.broadcasted_iota(jnp.int32, (GLA_SUB, t), 1)
        col_sub, col_chunk = col // GLA_SUB, col // GLA_CHUNK
        causal = (col - col_sub * GLA_SUB >= row) if reverse else (col - col_sub * GLA_SUB <= row)
        a_rows = []
        for i in range(n_sub):
            acc = jnp.zeros((GLA_SUB, t), F32)
            for n, (pi, pj) in enumerate(near):
                if pi == i:
                    sel = (col_sub == pj) & causal if pj == i else col_sub == pj
                    acc = jnp.where(sel, rows(r_near, GLA_SUB, n), acc)
            for n, (pi, pj) in enumerate(far):
                if pi == i // sub_per_chunk:
                    piece = rows(rows(r_far, GLA_CHUNK, n), GLA_SUB, i % sub_per_chunk)
                    acc = jnp.where(col_chunk == pj, piece, acc)
            a_rows.append(acc)
        o = o + _dot(jnp.concatenate(a_rows, axis=0).astype(BF16), v)

    kd = jnp.concatenate(
        [rows(k_chunk, GLA_CHUNK, j) * jnp.exp(b_end - e_chunk[j]) for j in range(n_chunk)], axis=0)
    st_ref[...] = st_ref[...] * jnp.exp(b_end) + _dot_tn(v, kd.astype(BF16))
    return o


def _gla_scan_kernel(*refs, with_out, q_scale):
    it = iter(refs)
    if with_out:
        qf_ref, qb_ref = next(it), next(it)
    kf_ref, kb_ref, vf_ref, vb_ref, zf_ref, zb_ref = (next(it) for _ in range(6))
    wdf_ref, bdf_ref, wdb_ref, bdb_ref, s0f_ref, s0b_ref = (next(it) for _ in range(6))
    if with_out:
        of_ref, ob_ref = next(it), next(it)
    else:
        sf_out, sb_out = next(it), next(it)
    sf_scr, sb_scr = next(it), next(it)

    step = pl.program_id(2)

    @pl.when(step == 0)
    def _():
        sf_scr[...] = s0f_ref[...]
        sb_scr[...] = s0b_ref[...]

    bf = _cumsum_rows(_gate_log_decay(zf_ref[...], wdf_ref[...], bdf_ref[...]), False)
    bb = _cumsum_rows(_gate_log_decay(zb_ref[...], wdb_ref[...], bdb_ref[...]), True)

    o = _gla_block(qf_ref[...].astype(F32) * q_scale if with_out else None,
                   kf_ref[...].astype(F32), vf_ref[...], bf, sf_scr, reverse=False)
    if with_out:
        of_ref[...] = o.astype(of_ref.dtype)
    o = _gla_block(qb_ref[...].astype(F32) * q_scale if with_out else None,
                   kb_ref[...].astype(F32), vb_ref[...], bb, sb_scr, reverse=True)
    if with_out:
        ob_ref[...] = o.astype(ob_ref.dtype)

    if not with_out:
        @pl.when(step == pl.num_programs(2) - 1)
        def _():
            sf_out[...] = sf_scr[...]
            sb_out[...] = sb_scr[...]


def _gla_scan(p, zg, wdf, bdf, wdb, bdb, s0f, s0b, *, with_out, q_col, k_col, v_col,
              dk, dv, tb, name):
    bsz, length, _ = p.shape
    nh = GLA_HEADS
    n = length // tb
    zw = zg.shape[-1]
    kq0, kk0, kv0 = q_col // dk, k_col // dk, v_col // dv

    def fwd(c):
        return c

    def bwd(c):
        return n - 1 - c

    def tile(width, c0, order):
        return pl.BlockSpec((None, tb, width), lambda b, h, c: (b, order(c), c0 + h))

    in_specs, args = [], []
    if with_out:
        in_specs += [tile(dk, kq0, fwd), tile(dk, kq0, bwd)]
        args += [p, p]
    in_specs += [tile(dk, kk0, fwd), tile(dk, kk0, bwd),
                 tile(dv, kv0, fwd), tile(dv, kv0, bwd),
                 pl.BlockSpec((None, tb, zw), lambda b, h, c: (b, c, 0)),
                 pl.BlockSpec((None, tb, zw), lambda b, h, c: (b, n - 1 - c, 0)),
                 pl.BlockSpec((zw, dk), lambda b, h, c: (0, h)),
                 pl.BlockSpec((1, dk), lambda b, h, c: (0, h)),
                 pl.BlockSpec((zw, dk), lambda b, h, c: (0, h)),
                 pl.BlockSpec((1, dk), lambda b, h, c: (0, h)),
                 pl.BlockSpec((None, None, dv, dk), lambda b, h, c: (b, h, 0, 0)),
                 pl.BlockSpec((None, None, dv, dk), lambda b, h, c: (b, h, 0, 0))]
    args += [p, p, p, p, zg, zg, wdf, bdf, wdb, bdb, s0f, s0b]

    if with_out:
        out_shape = (jax.ShapeDtypeStruct((bsz, length, nh * dv), BF16),) * 2
        out_specs = (pl.BlockSpec((None, tb, dv), lambda b, h, c: (b, c, h)),
                     pl.BlockSpec((None, tb, dv), lambda b, h, c: (b, n - 1 - c, h)))
    else:
        out_shape = (jax.ShapeDtypeStruct((bsz, nh, dv, dk), F32),) * 2
        out_specs = (pl.BlockSpec((None, None, dv, dk), lambda b, h, c: (b, h, 0, 0)),) * 2

    return pl.pallas_call(
        functools.partial(_gla_scan_kernel, with_out=with_out, q_scale=dk ** -0.5),
        out_shape=out_shape,
        grid=(bsz, nh, n),
        in_specs=in_specs,
        out_specs=out_specs,
        scratch_shapes=[pltpu.VMEM((dv, dk), F32), pltpu.VMEM((dv, dk), F32)],
        compiler_params=_cparams(3),
        name=name,
    )(*args)


def _gla_out_kernel(of_ref, ob_ref, r_ref, og_ref, w_ref, h_ref, gate_ref, o_ref, *, dv):
    o = of_ref[...].astype(F32) + ob_ref[...].astype(F32)
    heads = []
    for hh in range(GLA_HEADS):
        oh = o[:, hh * dv:(hh + 1) * dv]
        heads.append(oh * lax.rsqrt(jnp.mean(oh * oh, axis=-1, keepdims=True) + EPS))
    o = jnp.concatenate(heads, axis=-1) * og_ref[...]
    y = (o * _silu(r_ref[...].astype(F32))).astype(BF16)
    o_ref[...] = h_ref[...] + gate_ref[...] * _dot(y, w_ref[...])


def _gla_out(of, ob, p, og, w_out, h, mod, *, layer, row_of_tile, tm, dv):
    t, d = h.shape
    n_in = w_out.shape[0]
    return pl.pallas_call(
        functools.partial(_gla_out_kernel, dv=dv),
        out_shape=jax.ShapeDtypeStruct((t, d), F32),
        grid=(t // tm,),
        in_specs=[
            pl.BlockSpec((tm, n_in), lambda i: (i, 0)),
            pl.BlockSpec((tm, n_in), lambda i: (i, 0)),
            pl.BlockSpec((tm, n_in), lambda i: (i, 0)),
            pl.BlockSpec((1, n_in), lambda i: (0, 0)),
            pl.BlockSpec((n_in, d), lambda i: (0, 0), pipeline_mode=pl.Buffered(1)),
            pl.BlockSpec((tm, d), lambda i: (i, 0)),
            _mod_spec(layer, 5, row_of_tile, d),
        ],
        out_specs=pl.BlockSpec((tm, d), lambda i: (i, 0)),
        compiler_params=_cparams(1),
        name="gla_out",
    )(of, ob, p, og, w_out, h, mod)


def _pw1_kernel(z_ref, shift_ref, scale_ref, g_ref, wa_ref, wg_ref, ba_ref, bg_ref, o_ref, u_scr):
    j = pl.program_id(1)

    @pl.when(j == 0)
    def _():
        u_scr[...] = _adanorm(z_ref[...], g_ref[...], shift_ref[...], scale_ref[...]).astype(BF16)

    u = u_scr[...]
    a = _dot(u, wa_ref[...]) + ba_ref[...]
    gt = _dot(u, wg_ref[...]) + bg_ref[...]
    o_ref[...] = a * jax.nn.sigmoid(gt)


def _pw1(z, mod, gains, w, b, *, layer, row_of_tile, tm, tn):
    t, d = z.shape
    n_half = w.shape[1] // 2
    nb = n_half // tn
    return pl.pallas_call(
        _pw1_kernel,
        out_shape=jax.ShapeDtypeStruct((t, n_half), F32),
        grid=(t // tm, nb),
        in_specs=[
            pl.BlockSpec((tm, d), lambda i, j: (i, 0)),
            _mod_spec(layer, 3, row_of_tile, d),
            _mod_spec(layer, 4, row_of_tile, d),
            _gain_spec(layer, 1, d),
            pl.BlockSpec((d, tn), lambda i, j: (0, j)),
            pl.BlockSpec((d, tn), lambda i, j: (0, nb + j)),
            pl.BlockSpec((1, tn), lambda i, j: (0, j)),
            pl.BlockSpec((1, tn), lambda i, j: (0, nb + j)),
        ],
        out_specs=pl.BlockSpec((tm, tn), lambda i, j: (i, j)),
        scratch_shapes=[pltpu.VMEM((tm, d), BF16)],
        compiler_params=_cparams(2),
        name="conv_pw1_glu",
    )(z, mod, mod, gains, w, w, b, b)


def _conv_kernel(zp_ref, zc_ref, zn_ref, wdw_ref, bdw_ref, lng_ref, lnb_ref, w2_ref, b2_ref,
                 h_ref, gate_ref, o_ref, ext_scr, dw_scr, *, width, rows_per_pass):
    i = pl.program_id(1)
    tt, d = zc_ref.shape
    halo = CONV_ROW_HALO
    pad = width // 2

    ext_scr[0:halo, :] = jnp.where(i > 0, zp_ref[...], 0.0)
    ext_scr[halo:halo + tt, :] = zc_ref[...]
    ext_scr[halo + tt:, :] = jnp.where(i < pl.num_programs(1) - 1, zn_ref[...], 0.0)

    first = halo - pad

    def strip(cs, carry):
        cols = pl.ds(pl.multiple_of(cs * LANE, LANE), LANE)
        for rb in range(tt // rows_per_pass):
            r0 = rb * rows_per_pass
            acc = jnp.zeros((rows_per_pass, LANE), F32)
            for res in range(SUBLANES):
                part = None
                for s in range(first, first + width):
                    if s % SUBLANES != res:
                        continue
                    src = r0 + s - res
                    term = (ext_scr[src:src + rows_per_pass + SUBLANES, cols]
                            * wdw_ref[s - first:s - first + 1, cols])
                    part = term if part is None else part + term
                acc = acc + part[res:res + rows_per_pass]
            dw_scr[r0:r0 + rows_per_pass, cols] = acc
        return carry

    lax.fori_loop(0, d // LANE, strip, 0)

    zc = dw_scr[...] + bdw_ref[...]
    mu = jnp.mean(zc, axis=-1, keepdims=True)
    zc = zc - mu
    var = jnp.mean(zc * zc, axis=-1, keepdims=True)
    zn = zc * lax.rsqrt(var + EPS) * lng_ref[...] + lnb_ref[...]
    y = _dot(_silu(zn).astype(BF16), w2_ref[...]) + b2_ref[...]
    o_ref[...] = h_ref[...] + gate_ref[...] * y


def _conv(z, wdw, bdw, lng, lnb, w2, b2, h, mod, *, layer, bsz, tt):
    t, d = h.shape
    length = t // bsz
    nt = length // tt
    width = wdw.shape[0]
    hb = tt // CONV_ROW_HALO
    n_halo = length // CONV_ROW_HALO
    z3 = z.reshape(bsz, length, d)
    vec = pl.BlockSpec((1, d), lambda b, i: (0, 0))
    return pl.pallas_call(
        functools.partial(_conv_kernel, width=width, rows_per_pass=64),
        out_shape=jax.ShapeDtypeStruct((bsz, length, d), F32),
        grid=(bsz, nt),
        in_specs=[
            pl.BlockSpec((None, CONV_ROW_HALO, d), lambda b, i: (b, jnp.maximum(i * hb - 1, 0), 0)),
            pl.BlockSpec((None, tt, d), lambda b, i: (b, i, 0)),
            pl.BlockSpec((None, CONV_ROW_HALO, d),
                         lambda b, i: (b, jnp.minimum((i + 1) * hb, n_halo - 1), 0)),
            pl.BlockSpec((width, d), lambda b, i: (0, 0)),
            vec, vec, vec,
            pl.BlockSpec((d, d), lambda b, i: (0, 0), pipeline_mode=pl.Buffered(1)),
            vec,
            pl.BlockSpec((None, tt, d), lambda b, i: (b, i, 0)),
            pl.BlockSpec((None, None, None, 1, d), lambda b, i: (layer, b, 5, 0, 0)),
        ],
        out_specs=pl.BlockSpec((None, tt, d), lambda b, i: (b, i, 0)),
        scratch_shapes=[pltpu.VMEM((tt + 2 * CONV_ROW_HALO, d), F32), pltpu.VMEM((tt, d), F32)],
        compiler_params=_cparams(2),
        name="conv_dw_ln_pw2",
    )(z3, z3, z3, wdw, bdw, lng, lnb, w2, b2, h.reshape(bsz, length, d), mod).reshape(t, d)


def _pos_embed_2d(n_tok, d, dtype):
    rows = n_tok // GRID_W
    quarter = d // 4
    omega = 1.0 / (10000.0 ** (jnp.arange(quarter, dtype=F32) / quarter))
    ar = jnp.arange(rows, dtype=F32)[:, None] * omega[None]
    ac = jnp.arange(GRID_W, dtype=F32)[:, None] * omega[None]
    row_part = jnp.concatenate([jnp.sin(ar), jnp.cos(ar)], axis=-1)[:, None, :]
    col_part = jnp.concatenate([jnp.sin(ac), jnp.cos(ac)], axis=-1)[None, :, :]
    shape = (rows, GRID_W, 2 * quarter)
    emb = jnp.concatenate([jnp.broadcast_to(row_part, shape), jnp.broadcast_to(col_part, shape)], axis=-1)
    return emb.reshape(n_tok, d).astype(dtype)


def _pick_tile(n, pref):
    t = min(pref, n)
    while n % t:
        t //= 2
    return t


def kernel(x, c, ctx, c_ctx, ada_w, ada_b, norm_g, final_norm_g, ffn_w_gate, ffn_w_up, ffn_w_down,
           gla_w_in, gla_w_decay_f, gla_b_decay_f, gla_w_decay_b, gla_b_decay_b, gla_out_norm_g,
           gla_w_out, conv_w_pw1, conv_b_pw1, conv_w_dw, conv_b_dw, conv_ln_g, conv_ln_b,
           conv_w_pw2, conv_b_pw2):
    bsz, length, d = x.shape
    ctx_len = ctx.shape[1]
    depth = ada_w.shape[0]
    assert depth == 2 and bsz + 1 <= MOD_ROWS
    f = ffn_w_gate.shape[-1]
    dk_all, dv_all = d // 2, d
    dk, dv = dk_all // GLA_HEADS, dv_all // GLA_HEADS
    rank = gla_w_decay_f.shape[1]
    t_lat, t_ctx = bsz * length, bsz * ctx_len

    gains = norm_g.reshape(depth, 3, 1, d)

    w_in = gla_w_in[0]
    qr = dk_all + dv_all
    w_main = jnp.concatenate([w_in[:, dk_all:qr], w_in[:, :dk_all], w_in[:, qr:qr + dk_all + dv_all]],
                             axis=1).astype(BF16)
    n_main = w_main.shape[1]
    assert 2 * rank == GATE_COPY
    wz1 = w_in[:, n_main:].astype(BF16)
    wz = jnp.concatenate([wz1, wz1, wz1, jnp.zeros((d, LANE - 3 * GATE_COPY), BF16)], axis=1)

    def gate_weight(w, first_row):
        hi = w.astype(BF16)
        lo = (w - hi.astype(F32)).astype(BF16)
        out = jnp.zeros((LANE, w.shape[1]), BF16)
        for copy, part in enumerate((hi, hi, lo)):
            out = lax.dynamic_update_slice(out, part, (copy * GATE_COPY + first_row, 0))
        return out

    wdf = gate_weight(gla_w_decay_f[0], 0)
    wdb = gate_weight(gla_w_decay_b[0], rank)
    bdf = gla_b_decay_f[0].reshape(1, dk_all)
    bdb = gla_b_decay_b[0].reshape(1, dk_all)
    og = gla_out_norm_g[0].reshape(1, dv_all)
    w_out = gla_w_out[0].astype(BF16)

    w_pw1 = conv_w_pw1[0].astype(BF16)
    b_pw1 = conv_b_pw1[0].reshape(1, -1)
    w_pw2 = conv_w_pw2[0].astype(BF16)

    cond = jnp.concatenate([c, c_ctx[None], jnp.zeros((MOD_ROWS - bsz - 1, d), F32)], axis=0)
    mod = _modulation(cond, ada_w, ada_b).reshape(depth, MOD_ROWS, N_MOD, 1, d)

    def lat_row(tm):
        return lambda i: i // (length // tm)

    ctx_row = lambda i: bsz

    wg, wu, wd = ffn_w_gate.astype(BF16), ffn_w_up.astype(BF16), ffn_w_down.astype(BF16)

    def ffn(zz, layer, which, tm, row, **kw):
        return _ffn(zz, mod, gains, wg, wu, wd, layer=layer, which=which,
                    row_of_tile=row, tm=tm, tf=FFN_SLICE, **kw)

    tm = _pick_tile(length, 512)
    tm_ffn = _pick_tile(length, 1024)
    tm_ctx = _pick_tile(t_ctx, 1024)

    pos = _pos_embed_2d(length, d, x.dtype)
    xf = x.reshape(t_lat, d)
    cf = ctx.reshape(t_ctx, d)

    h = ffn(xf, 0, 0, tm_ffn, lat_row(tm_ffn), pos=pos)
    hc = ffn(cf, 0, 0, tm_ctx, ctx_row)

    p, zg = _inproj(h, mod, gains, w_main, wz, layer=0, row_of_tile=lat_row(tm), tm=tm,
                    tn=_pick_tile(2 * dk_all, 2048), col0=0, n_out=n_main, name="gla_inproj")
    kv0 = dv_all + dk_all
    pc, zgc = _inproj(hc, mod, gains, w_main, wz, layer=0, row_of_tile=ctx_row, tm=_pick_tile(t_ctx, 512),
                      tn=_pick_tile(dk_all, 1024), col0=kv0, n_out=n_main - kv0, name="gla_inproj_ctx")

    zeros = jnp.zeros((bsz, GLA_HEADS, dv, dk), F32)
    tb_ctx = _pick_tile(ctx_len, GLA_BLOCK)
    sf, sb = _gla_scan(pc.reshape(bsz, ctx_len, -1), zgc.reshape(bsz, ctx_len, -1),
                       wdf, bdf, wdb, bdb, zeros, zeros, with_out=False,
                       q_col=0, k_col=0, v_col=dk_all, dk=dk, dv=dv, tb=tb_ctx, name="gla_scan_ctx")
    tb = _pick_tile(length, GLA_BLOCK)
    of, ob = _gla_scan(p.reshape(bsz, length, -1), zg.reshape(bsz, length, -1),
                       wdf, bdf, wdb, bdb, sf, sb, with_out=True,
                       q_col=dv_all, k_col=dv_all + dk_all, v_col=dv_all + 2 * dk_all,
                       dk=dk, dv=dv, tb=tb, name="gla_scan")
    tm_o = _pick_tile(length, 512)
    h = _gla_out(of.reshape(t_lat, dv_all), ob.reshape(t_lat, dv_all), p, og, w_out, h, mod,
                 layer=0, row_of_tile=lat_row(tm_o), tm=tm_o, dv=dv)
    h = ffn(h, 0, 1, tm_ffn, lat_row(tm_ffn))

    h = ffn(h, 1, 0, tm_ffn, lat_row(tm_ffn))
    z = _pw1(h, mod, gains, w_pw1, b_pw1, layer=1, row_of_tile=lat_row(tm), tm=tm, tn=_pick_tile(d, 1024))
    h = _conv(z, conv_w_dw[0], conv_b_dw[0].reshape(1, d), conv_ln_g[0].reshape(1, d),
              conv_ln_b[0].reshape(1, d), w_pw2, conv_b_pw2[0].reshape(1, d), h, mod,
              layer=1, bsz=bsz, tt=_pick_tile(length, 256))
    h = ffn(h, 1, 1, tm_ffn, lat_row(tm_ffn), final_gain=final_norm_g.reshape(1, d))
    return h.reshape(bsz, length, d)
```

```python
import functools

import jax
import jax.numpy as jnp
from jax import lax
from jax.experimental import pallas as pl
from jax.experimental.pallas import tpu as pltpu

F32 = jnp.float32
BF16 = jnp.bfloat16

EPS = 1e-6
N_MOD = 9
GRID_W = 64
GLA_HEADS = 4
GLA_GATE_TEMP = 16.0
LOG2_E = 1.4426950408889634
GLA_BLOCK = 256
GLA_CHUNK = 64
GLA_SUB = 16
GATE_COPY = 32
FFN_SLICE = 256
SUBLANES = 8
CONV_ROW_HALO = 16
MOD_ROWS = 8
V7X_VMEM_LIMIT = 56 * 1024 * 1024
LANE = 128


def _cparams(n_axes):
    return pltpu.CompilerParams(
        dimension_semantics=("arbitrary",) * n_axes,
        vmem_limit_bytes=V7X_VMEM_LIMIT)


def _dot(a, b):
    return jnp.dot(a, b, preferred_element_type=F32)


def _dot_nt(a, b):
    return lax.dot_general(a, b, (((1,), (1,)), ((), ())), preferred_element_type=F32)


def _dot_tn(a, b):
    return lax.dot_general(a, b, (((0,), (0,)), ((), ())), preferred_element_type=F32)


def _silu(x):
    return x * jax.nn.sigmoid(x)


def _adanorm(z, g, shift, scale):
    y = z * lax.rsqrt(jnp.mean(z * z, axis=-1, keepdims=True) + EPS)
    return y * (g * (1.0 + scale)) + shift


def _mod_kernel(c_ref, w_ref, b_ref, o_ref):
    s = _silu(c_ref[...]).astype(BF16)
    o_ref[...] = _dot(s, w_ref[...].astype(BF16)) + b_ref[...]


def _modulation(cond, ada_w, ada_b):
    depth, d, n = ada_w.shape
    tn = _pick_tile(n, 1024)
    return pl.pallas_call(
        _mod_kernel,
        out_shape=jax.ShapeDtypeStruct((depth, MOD_ROWS, n), F32),
        grid=(depth, n // tn),
        in_specs=[
            pl.BlockSpec((MOD_ROWS, d), lambda l, j: (0, 0)),
            pl.BlockSpec((None, d, tn), lambda l, j: (l, 0, j)),
            pl.BlockSpec((None, 1, tn), lambda l, j: (l, 0, j)),
        ],
        out_specs=pl.BlockSpec((None, MOD_ROWS, tn), lambda l, j: (l, 0, j)),
        compiler_params=_cparams(2),
        name="ada_modulation",
    )(cond, ada_w, ada_b.reshape(depth, 1, n))


def _mod_spec(layer, k, row_of_tile, d):
    return pl.BlockSpec((None, None, None, 1, d),
                        lambda i, *_: (layer, row_of_tile(i), k, 0, 0))


def _gain_spec(layer, k, d):
    return pl.BlockSpec((None, None, 1, d), lambda *_: (layer, k, 0, 0))


def _ffn_kernel(*refs, has_pos, final_norm, has_tail):
    it = iter(refs)
    z_ref = next(it)
    pos_ref = next(it) if has_pos else None
    shift_ref, scale_ref, gate_ref, g_ref = next(it), next(it), next(it), next(it)
    fg_ref = next(it) if final_norm else None
    wg_ref, wu_ref, wd_ref = next(it), next(it), next(it)
    if has_tail:
        wg_tail, wu_tail, wd_tail = next(it), next(it), next(it)
    o_ref = next(it)
    u_scr = next(it)

    j = pl.program_id(1)

    def residual_in():
        return z_ref[...] + pos_ref[...] if has_pos else z_ref[...]

    def swiglu_down(u, wg, wu, wd):
        a = _dot(u, wg[...].astype(BF16))
        b = _dot(u, wu[...].astype(BF16))
        return _dot((_silu(a) * b).astype(BF16), wd[...].astype(BF16))

    @pl.when(j == 0)
    def _():
        u = _adanorm(residual_in(), g_ref[...], shift_ref[...], scale_ref[...]).astype(BF16)
        u_scr[...] = u
        o_ref[...] = swiglu_down(u, wg_tail, wu_tail, wd_tail) if has_tail else jnp.zeros_like(o_ref)

    o_ref[...] += swiglu_down(u_scr[...], wg_ref, wu_ref, wd_ref)

    @pl.when(j == pl.num_programs(1) - 1)
    def _():
        out = residual_in() + (0.5 * gate_ref[...]) * o_ref[...]
        if final_norm:
            out = out * lax.rsqrt(jnp.mean(out * out, axis=-1, keepdims=True) + EPS) * fg_ref[...]
        o_ref[...] = out


def _ffn(z, mod, gains, wg, wu, wd, *, layer, which, row_of_tile, tm, tf,
         pos=None, final_gain=None):
    t, d = z.shape
    f = wg.shape[-1]
    tail = f % tf
    n_main = f // tf
    has_tail = tail > 0
    assert tail % LANE == 0 and (not has_tail or (n_main * tf) % tail == 0)
    base = 0 if which == 0 else 6
    gain_k = 0 if which == 0 else 2
    has_pos = pos is not None
    final_norm = final_gain is not None
    once = pl.Buffered(1)

    in_specs = [pl.BlockSpec((tm, d), lambda i, j: (i, 0), pipeline_mode=once)]
    args = [z]
    if has_pos:
        n_pos = pos.shape[0] // tm
        in_specs.append(pl.BlockSpec((tm, d), lambda i, j: (i % n_pos, 0), pipeline_mode=once))
        args.append(pos)
    for k in range(3):
        in_specs.append(_mod_spec(layer, base + k, row_of_tile, d))
        args.append(mod)
    in_specs.append(_gain_spec(layer, gain_k, d))
    args.append(gains)
    if final_norm:
        in_specs.append(pl.BlockSpec((1, d), lambda i, j: (0, 0)))
        args.append(final_gain)
    in_specs += [
        pl.BlockSpec((None, None, d, tf), lambda i, j: (layer, which, 0, j)),
        pl.BlockSpec((None, None, d, tf), lambda i, j: (layer, which, 0, j)),
        pl.BlockSpec((None, None, tf, d), lambda i, j: (layer, which, j, 0)),
    ]
    args += [wg, wu, wd]
    if has_tail:
        tail_blk = n_main * tf // tail
        in_specs += [
            pl.BlockSpec((None, None, d, tail), lambda i, j: (layer, which, 0, tail_blk), pipeline_mode=once),
            pl.BlockSpec((None, None, d, tail), lambda i, j: (layer, which, 0, tail_blk), pipeline_mode=once),
            pl.BlockSpec((None, None, tail, d), lambda i, j: (layer, which, tail_blk, 0), pipeline_mode=once),
        ]
        args += [wg, wu, wd]

    return pl.pallas_call(
        functools.partial(_ffn_kernel, has_pos=has_pos, final_norm=final_norm, has_tail=has_tail),
        out_shape=jax.ShapeDtypeStruct((t, d), F32),
        grid=(t // tm, n_main),
        in_specs=in_specs,
        out_specs=pl.BlockSpec((tm, d), lambda i, j: (i, 0), pipeline_mode=once if has_pos else None),
        scratch_shapes=[pltpu.VMEM((tm, d), BF16)],
        compiler_params=_cparams(2),
        name=f"ffn_l{layer}_{which}",
    )(*args)


def _inproj_kernel(z_ref, shift_ref, scale_ref, g_ref, w_ref, wz_ref, o_ref, oz_ref, u_scr):
    j = pl.program_id(1)

    @pl.when(j == 0)
    def _():
        u = _adanorm(z_ref[...], g_ref[...], shift_ref[...], scale_ref[...]).astype(BF16)
        u_scr[...] = u
        oz_ref[...] = _dot(u, wz_ref[...])

    o_ref[...] = _dot(u_scr[...], w_ref[...]).astype(o_ref.dtype)


def _inproj(z, mod, gains, w, wz, *, layer, row_of_tile, tm, tn, col0, n_out, name):
    t, d = z.shape
    cb0 = col0 // tn
    return pl.pallas_call(
        _inproj_kernel,
        out_shape=(jax.ShapeDtypeStruct((t, n_out), BF16),
                   jax.ShapeDtypeStruct((t, wz.shape[1]), F32)),
        grid=(t // tm, n_out // tn),
        in_specs=[
            pl.BlockSpec((tm, d), lambda i, j: (i, 0)),
            _mod_spec(layer, 3, row_of_tile, d),
            _mod_spec(layer, 4, row_of_tile, d),
            _gain_spec(layer, 1, d),
            pl.BlockSpec((d, tn), lambda i, j: (0, cb0 + j)),
            pl.BlockSpec(wz.shape, lambda i, j: (0, 0)),
        ],
        out_specs=(pl.BlockSpec((tm, tn), lambda i, j: (i, j)),
                   pl.BlockSpec((tm, wz.shape[1]), lambda i, j: (i, 0))),
        scratch_shapes=[pltpu.VMEM((tm, d), BF16)],
        compiler_params=_cparams(2),
        name=name,
    )(z, mod, mod, gains, w, wz)


def _log_sigmoid(x):
    return jnp.minimum(x, 0.0) - jnp.log(1.0 + jnp.exp(-jnp.abs(x)))


def _split_bf16(x, parts):
    out = []
    for _ in range(parts - 1):
        hi = x.astype(BF16)
        out.append(hi)
        x = x - hi.astype(F32)
    out.append(x.astype(BF16))
    return out


def _gate_features(z):
    lane = lax.broadcasted_iota(jnp.int32, z.shape, 1)
    is_lo = (lane >= GATE_COPY) & (lane < 2 * GATE_COPY)
    return jnp.where(is_lo, z - z.astype(BF16).astype(F32), z).astype(BF16)


def _gate_log_decay(zcat, w, bias):
    return _log_sigmoid(_dot(zcat, w) + bias) * (LOG2_E / GLA_GATE_TEMP)


def _time_tri(t, reverse):
    row = lax.broadcasted_iota(jnp.int32, (t, t), 0)
    col = lax.broadcasted_iota(jnp.int32, (t, t), 1)
    return jnp.where((col >= row) if reverse else (col <= row), 1.0, 0.0).astype(BF16)


def _cumsum_rows(tri, g):
    g_hi, g_lo = _split_bf16(g, 2)
    return _dot(tri, g_hi) + _dot(tri, g_lo)


def _score_masks(t, reverse):
    row = lax.broadcasted_iota(jnp.int32, (GLA_SUB, t), 0)
    col = lax.broadcasted_iota(jnp.int32, (GLA_SUB, t), 1)
    col_sub, col_chunk = col // GLA_SUB, col // GLA_CHUNK
    within = col - col_sub * GLA_SUB
    causal = (within >= row) if reverse else (within <= row)
    return col_sub, col_chunk, causal


def _gla_block(q, k, v, b_ref, st_ref, masks, *, reverse):
    t = b_ref.shape[0]
    n_sub, n_chunk, sub_per_chunk = t // GLA_SUB, t // GLA_CHUNK, GLA_CHUNK // GLA_SUB

    def rows(x, size, i):
        return x[i * size:(i + 1) * size]

    def b_rows(size, i):
        return b_ref[i * size:(i + 1) * size, :]

    def far_edge(size, j):
        r = j * size if reverse else (j + 1) * size - 1
        return b_ref[r:r + 1, :]

    b_end = far_edge(t, 0)
    e_chunk = [far_edge(GLA_CHUNK, j) for j in range(n_chunk)]
    k_chunk = jnp.concatenate(
        [rows(k, GLA_CHUNK, j) * jnp.exp2(e_chunk[j] - b_rows(GLA_CHUNK, j)) for j in range(n_chunk)], axis=0)

    o = None
    if q is not None:
        o = _dot_nt((q * jnp.exp2(b_ref[...])).astype(BF16), st_ref[...].astype(BF16))
        e_sub = [far_edge(GLA_SUB, j) for j in range(n_sub)]
        k_sub = jnp.concatenate(
            [rows(k, GLA_SUB, j) * jnp.exp2(e_sub[j] - b_rows(GLA_SUB, j)) for j in range(n_sub)], axis=0)

        def visible(i, n):
            return range(i, n) if reverse else range(i + 1)

        near = [(i, j) for i in range(n_sub) for j in visible(i, n_sub)
                if j // sub_per_chunk == i // sub_per_chunk]
        far = [(i, j) for i in range(n_chunk) for j in visible(i, n_chunk) if j != i]
        lhs_near = jnp.concatenate(
            [rows(q, GLA_SUB, i) * jnp.exp2(b_rows(GLA_SUB, i) - e_sub[j]) for i, j in near], axis=0)
        lhs_far = jnp.concatenate(
            [rows(q, GLA_CHUNK, i) * jnp.exp2(b_rows(GLA_CHUNK, i) - e_chunk[j]) for i, j in far], axis=0)
        r_near = _dot_nt(lhs_near.astype(BF16), k_sub.astype(BF16))
        r_far = _dot_nt(lhs_far.astype(BF16), k_chunk.astype(BF16))

        col_sub, col_chunk, causal = masks
        a_rows = []
        for i in range(n_sub):
            acc = jnp.zeros((GLA_SUB, t), F32)
            for n, (pi, pj) in enumerate(near):
                if pi == i:
                    sel = (col_sub == pj) & causal if pj == i else col_sub == pj
                    acc = jnp.where(sel, rows(r_near, GLA_SUB, n), acc)
            for n, (pi, pj) in enumerate(far):
                if pi == i // sub_per_chunk:
                    piece = rows(rows(r_far, GLA_CHUNK, n), GLA_SUB, i % sub_per_chunk)
                    acc = jnp.where(col_chunk == pj, piece, acc)
            a_rows.append(acc)
        o = o + _dot(jnp.concatenate(a_rows, axis=0).astype(BF16), v)

    kd = jnp.concatenate(
        [rows(k_chunk, GLA_CHUNK, j) * jnp.exp2(b_end - e_chunk[j]) for j in range(n_chunk)], axis=0)
    st_ref[...] = st_ref[...] * jnp.exp2(b_end) + _dot_tn(v, kd.astype(BF16))
    return o


def _gla_scan_kernel(*refs, with_out, q_scale, heads, dk, dv):
    it = iter(refs)
    if with_out:
        qf_ref, qb_ref = next(it), next(it)
    kf_ref, kb_ref, vf_ref, vb_ref, zf_ref, zb_ref = (next(it) for _ in range(6))
    wdf_ref, bdf_ref, wdb_ref, bdb_ref, s0f_ref, s0b_ref = (next(it) for _ in range(6))
    if with_out:
        of_ref, ob_ref = next(it), next(it)
    else:
        sf_out, sb_out = next(it), next(it)
    sf_scr, sb_scr, bf_scr, bb_scr = next(it), next(it), next(it), next(it)

    step = pl.program_id(2)
    tb = zf_ref.shape[0]

    @pl.when(step == 0)
    def _():
        sf_scr[...] = s0f_ref[...]
        sb_scr[...] = s0b_ref[...]

    def direction(q_ref, k_ref, v_ref, z_ref, wd_ref, bd_ref, b_scr, s_scr, o_ref, reverse):
        zcat = _gate_features(z_ref[...])
        tri = _time_tri(tb, reverse)
        masks = _score_masks(tb, reverse) if with_out else None
        for h in range(heads):
            ck, cv = slice(h * dk, (h + 1) * dk), slice(h * dv, (h + 1) * dv)
            b_scr[h] = _cumsum_rows(tri, _gate_log_decay(zcat, wd_ref[:, ck], bd_ref[:, ck]))
            o = _gla_block(q_ref[:, ck].astype(F32) * q_scale if with_out else None,
                           k_ref[:, ck].astype(F32), v_ref[:, cv], b_scr.at[h], s_scr.at[h], masks,
                           reverse=reverse)
            if with_out:
                o_ref[:, cv] = o.astype(o_ref.dtype)

    direction(qf_ref if with_out else None, kf_ref, vf_ref, zf_ref, wdf_ref, bdf_ref, bf_scr, sf_scr,
              of_ref if with_out else None, False)
    direction(qb_ref if with_out else None, kb_ref, vb_ref, zb_ref, wdb_ref, bdb_ref, bb_scr, sb_scr,
              ob_ref if with_out else None, True)

    if not with_out:
        @pl.when(step == pl.num_programs(2) - 1)
        def _():
            sf_out[...] = sf_scr[...]
            sb_out[...] = sb_scr[...]


def _gla_scan(p, zg, wdf, bdf, wdb, bdb, s0f, s0b, *, with_out, q_col, k_col, v_col,
              dk, dv, tb, heads, name):
    bsz, length, _ = p.shape
    nh = GLA_HEADS
    n = length // tb
    zw = zg.shape[-1]
    wk, wv = heads * dk, heads * dv
    assert nh % heads == 0 and q_col % wk == 0 and k_col % wk == 0 and v_col % wv == 0
    kq0, kk0, kv0 = q_col // wk, k_col // wk, v_col // wv

    def fwd(c):
        return c

    def bwd(c):
        return n - 1 - c

    def tile(width, c0, order):
        return pl.BlockSpec((None, tb, width), lambda b, g, c: (b, order(c), c0 + g))

    state = pl.BlockSpec((None, heads, dv, dk), lambda b, g, c: (b, g, 0, 0))
    in_specs, args = [], []
    if with_out:
        in_specs += [tile(wk, kq0, fwd), tile(wk, kq0, bwd)]
        args += [p, p]
    in_specs += [tile(wk, kk0, fwd), tile(wk, kk0, bwd),
                 tile(wv, kv0, fwd), tile(wv, kv0, bwd),
                 pl.BlockSpec((None, tb, zw), lambda b, g, c: (b, c, 0)),
                 pl.BlockSpec((None, tb, zw), lambda b, g, c: (b, n - 1 - c, 0)),
                 pl.BlockSpec((zw, wk), lambda b, g, c: (0, g)),
                 pl.BlockSpec((1, wk), lambda b, g, c: (0, g)),
                 pl.BlockSpec((zw, wk), lambda b, g, c: (0, g)),
                 pl.BlockSpec((1, wk), lambda b, g, c: (0, g)),
                 state, state]
    args += [p, p, p, p, zg, zg, wdf, bdf, wdb, bdb, s0f, s0b]

    if with_out:
        out_shape = (jax.ShapeDtypeStruct((bsz, length, nh * dv), BF16),) * 2
        out_specs = (pl.BlockSpec((None, tb, wv), lambda b, g, c: (b, c, g)),
                     pl.BlockSpec((None, tb, wv), lambda b, g, c: (b, n - 1 - c, g)))
    else:
        out_shape = (jax.ShapeDtypeStruct((bsz, nh, dv, dk), F32),) * 2
        out_specs = (state, state)

    return pl.pallas_call(
        functools.partial(_gla_scan_kernel, with_out=with_out, q_scale=dk ** -0.5,
                          heads=heads, dk=dk, dv=dv),
        out_shape=out_shape,
        grid=(bsz, nh // heads, n),
        in_specs=in_specs,
        out_specs=out_specs,
        scratch_shapes=[pltpu.VMEM((heads, dv, dk), F32), pltpu.VMEM((heads, dv, dk), F32),
                        pltpu.VMEM((heads, tb, dk), F32), pltpu.VMEM((heads, tb, dk), F32)],
        compiler_params=_cparams(3),
        name=name,
    )(*args)


def _gla_out_kernel(of_ref, ob_ref, r_ref, og_ref, w_ref, h_ref, gate_ref, o_ref, *, dv):
    o = of_ref[...].astype(F32) + ob_ref[...].astype(F32)
    heads = []
    for hh in range(GLA_HEADS):
        oh = o[:, hh * dv:(hh + 1) * dv]
        heads.append(oh * lax.rsqrt(jnp.mean(oh * oh, axis=-1, keepdims=True) + EPS))
    o = jnp.concatenate(heads, axis=-1) * og_ref[...]
    y = (o * _silu(r_ref[...].astype(F32))).astype(BF16)
    o_ref[...] = h_ref[...] + gate_ref[...] * _dot(y, w_ref[...])


def _gla_out(of, ob, p, og, w_out, h, mod, *, layer, row_of_tile, tm, dv):
    t, d = h.shape
    n_in = w_out.shape[0]
    return pl.pallas_call(
        functools.partial(_gla_out_kernel, dv=dv),
        out_shape=jax.ShapeDtypeStruct((t, d), F32),
        grid=(t // tm,),
        in_specs=[
            pl.BlockSpec((tm, n_in), lambda i: (i, 0)),
            pl.BlockSpec((tm, n_in), lambda i: (i, 0)),
            pl.BlockSpec((tm, n_in), lambda i: (i, 0)),
            pl.BlockSpec((1, n_in), lambda i: (0, 0)),
            pl.BlockSpec((n_in, d), lambda i: (0, 0), pipeline_mode=pl.Buffered(1)),
            pl.BlockSpec((tm, d), lambda i: (i, 0)),
            _mod_spec(layer, 5, row_of_tile, d),
        ],
        out_specs=pl.BlockSpec((tm, d), lambda i: (i, 0)),
        compiler_params=_cparams(1),
        name="gla_out",
    )(of, ob, p, og, w_out, h, mod)


def _pw1_kernel(z_ref, shift_ref, scale_ref, g_ref, wa_ref, wg_ref, ba_ref, bg_ref, o_ref, u_scr):
    j = pl.program_id(1)

    @pl.when(j == 0)
    def _():
        u_scr[...] = _adanorm(z_ref[...], g_ref[...], shift_ref[...], scale_ref[...]).astype(BF16)

    u = u_scr[...]
    a = _dot(u, wa_ref[...]) + ba_ref[...]
    gt = _dot(u, wg_ref[...]) + bg_ref[...]
    o_ref[...] = a * jax.nn.sigmoid(gt)


def _pw1(z, mod, gains, w, b, *, layer, row_of_tile, tm, tn):
    t, d = z.shape
    n_half = w.shape[1] // 2
    nb = n_half // tn
    return pl.pallas_call(
        _pw1_kernel,
        out_shape=jax.ShapeDtypeStruct((t, n_half), F32),
        grid=(t // tm, nb),
        in_specs=[
            pl.BlockSpec((tm, d), lambda i, j: (i, 0)),
            _mod_spec(layer, 3, row_of_tile, d),
            _mod_spec(layer, 4, row_of_tile, d),
            _gain_spec(layer, 1, d),
            pl.BlockSpec((d, tn), lambda i, j: (0, j)),
            pl.BlockSpec((d, tn), lambda i, j: (0, nb + j)),
            pl.BlockSpec((1, tn), lambda i, j: (0, j)),
            pl.BlockSpec((1, tn), lambda i, j: (0, nb + j)),
        ],
        out_specs=pl.BlockSpec((tm, tn), lambda i, j: (i, j)),
        scratch_shapes=[pltpu.VMEM((tm, d), BF16)],
        compiler_params=_cparams(2),
        name="conv_pw1_glu",
    )(z, mod, mod, gains, w, w, b, b)


def _conv_kernel(zp_ref, zc_ref, zn_ref, wdw_ref, bdw_ref, lng_ref, lnb_ref, w2_ref, b2_ref,
                 h_ref, gate_ref, o_ref, ext_scr, dw_scr, *, width, rows_per_pass):
    i = pl.program_id(1)
    tt, d = zc_ref.shape
    halo = CONV_ROW_HALO
    pad = width // 2

    ext_scr[0:halo, :] = jnp.where(i > 0, zp_ref[...], 0.0)
    ext_scr[halo:halo + tt, :] = zc_ref[...]
    ext_scr[halo + tt:, :] = jnp.where(i < pl.num_programs(1) - 1, zn_ref[...], 0.0)

    first = halo - pad

    def strip(cs, carry):
        cols = pl.ds(pl.multiple_of(cs * LANE, LANE), LANE)
        for rb in range(tt // rows_per_pass):
            r0 = rb * rows_per_pass
            acc = jnp.zeros((rows_per_pass, LANE), F32)
            for res in range(SUBLANES):
                part = None
                for s in range(first, first + width):
                    if s % SUBLANES != res:
                        continue
                    src = r0 + s - res
                    term = (ext_scr[src:src + rows_per_pass + SUBLANES, cols]
                            * wdw_ref[s - first:s - first + 1, cols])
                    part = term if part is None else part + term
                acc = acc + part[res:res + rows_per_pass]
            dw_scr[r0:r0 + rows_per_pass, cols] = acc
        return carry

    lax.fori_loop(0, d // LANE, strip, 0)

    zc = dw_scr[...] + bdw_ref[...]
    mu = jnp.mean(zc, axis=-1, keepdims=True)
    zc = zc - mu
    var = jnp.mean(zc * zc, axis=-1, keepdims=True)
    zn = zc * lax.rsqrt(var + EPS) * lng_ref[...] + lnb_ref[...]
    y = _dot(_silu(zn).astype(BF16), w2_ref[...]) + b2_ref[...]
    o_ref[...] = h_ref[...] + gate_ref[...] * y


def _conv(z, wdw, bdw, lng, lnb, w2, b2, h, mod, *, layer, bsz, tt):
    t, d = h.shape
    length = t // bsz
    nt = length // tt
    width = wdw.shape[0]
    hb = tt // CONV_ROW_HALO
    n_halo = length // CONV_ROW_HALO
    z3 = z.reshape(bsz, length, d)
    vec = pl.BlockSpec((1, d), lambda b, i: (0, 0))
    return pl.pallas_call(
        functools.partial(_conv_kernel, width=width, rows_per_pass=64),
        out_shape=jax.ShapeDtypeStruct((bsz, length, d), F32),
        grid=(bsz, nt),
        in_specs=[
            pl.BlockSpec((None, CONV_ROW_HALO, d), lambda b, i: (b, jnp.maximum(i * hb - 1, 0), 0)),
            pl.BlockSpec((None, tt, d), lambda b, i: (b, i, 0)),
            pl.BlockSpec((None, CONV_ROW_HALO, d),
                         lambda b, i: (b, jnp.minimum((i + 1) * hb, n_halo - 1), 0)),
            pl.BlockSpec((width, d), lambda b, i: (0, 0)),
            vec, vec, vec,
            pl.BlockSpec((d, d), lambda b, i: (0, 0), pipeline_mode=pl.Buffered(1)),
            vec,
            pl.BlockSpec((None, tt, d), lambda b, i: (b, i, 0)),
            pl.BlockSpec((None, None, None, 1, d), lambda b, i: (layer, b, 5, 0, 0)),
        ],
        out_specs=pl.BlockSpec((None, tt, d), lambda b, i: (b, i, 0)),
        scratch_shapes=[pltpu.VMEM((tt + 2 * CONV_ROW_HALO, d), F32), pltpu.VMEM((tt, d), F32)],
        compiler_params=_cparams(2),
        name="conv_dw_ln_pw2",
    )(z3, z3, z3, wdw, bdw, lng, lnb, w2, b2, h.reshape(bsz, length, d), mod).reshape(t, d)


def _pos_embed_2d(n_tok, d, dtype):
    rows = n_tok // GRID_W
    quarter = d // 4
    omega = 1.0 / (10000.0 ** (jnp.arange(quarter, dtype=F32) / quarter))
    ar = jnp.arange(rows, dtype=F32)[:, None] * omega[None]
    ac = jnp.arange(GRID_W, dtype=F32)[:, None] * omega[None]
    row_part = jnp.concatenate([jnp.sin(ar), jnp.cos(ar)], axis=-1)[:, None, :]
    col_part = jnp.concatenate([jnp.sin(ac), jnp.cos(ac)], axis=-1)[None, :, :]
    shape = (rows, GRID_W, 2 * quarter)
    emb = jnp.concatenate([jnp.broadcast_to(row_part, shape), jnp.broadcast_to(col_part, shape)], axis=-1)
    return emb.reshape(n_tok, d).astype(dtype)


def _pick_tile(n, pref):
    t = min(pref, n)
    while n % t:
        t //= 2
    return t


def kernel(x, c, ctx, c_ctx, ada_w, ada_b, norm_g, final_norm_g, ffn_w_gate, ffn_w_up, ffn_w_down,
           gla_w_in, gla_w_decay_f, gla_b_decay_f, gla_w_decay_b, gla_b_decay_b, gla_out_norm_g,
           gla_w_out, conv_w_pw1, conv_b_pw1, conv_w_dw, conv_b_dw, conv_ln_g, conv_ln_b,
           conv_w_pw2, conv_b_pw2):
    bsz, length, d = x.shape
    ctx_len = ctx.shape[1]
    depth = ada_w.shape[0]
    assert depth == 2 and bsz + 1 <= MOD_ROWS
    f = ffn_w_gate.shape[-1]
    dk_all, dv_all = d // 2, d
    dk, dv = dk_all // GLA_HEADS, dv_all // GLA_HEADS
    rank = gla_w_decay_f.shape[1]
    t_lat, t_ctx = bsz * length, bsz * ctx_len

    gains = norm_g.reshape(depth, 3, 1, d)

    w_in = gla_w_in[0]
    qr = dk_all + dv_all
    w_main = jnp.concatenate([w_in[:, dk_all:qr], w_in[:, qr + dk_all:qr + dk_all + dv_all],
                              w_in[:, :dk_all], w_in[:, qr:qr + dk_all]], axis=1).astype(BF16)
    n_main = w_main.shape[1]
    v_col, q_col, k_col = dv_all, 2 * dv_all, 2 * dv_all + dk_all
    assert 2 * rank == GATE_COPY
    wz1 = w_in[:, n_main:].astype(BF16)
    wz = jnp.concatenate([wz1, wz1, wz1, jnp.zeros((d, LANE - 3 * GATE_COPY), BF16)], axis=1)

    def gate_weight(w, first_row):
        hi = w.astype(BF16)
        lo = (w - hi.astype(F32)).astype(BF16)
        out = jnp.zeros((LANE, w.shape[1]), BF16)
        for copy, part in enumerate((hi, hi, lo)):
            out = lax.dynamic_update_slice(out, part, (copy * GATE_COPY + first_row, 0))
        return out

    wdf = gate_weight(gla_w_decay_f[0], 0)
    wdb = gate_weight(gla_w_decay_b[0], rank)
    bdf = gla_b_decay_f[0].reshape(1, dk_all)
    bdb = gla_b_decay_b[0].reshape(1, dk_all)
    og = gla_out_norm_g[0].reshape(1, dv_all)
    w_out = gla_w_out[0].astype(BF16)

    w_pw1 = conv_w_pw1[0].astype(BF16)
    b_pw1 = conv_b_pw1[0].reshape(1, -1)
    w_pw2 = conv_w_pw2[0].astype(BF16)

    cond = jnp.concatenate([c, c_ctx[None], jnp.zeros((MOD_ROWS - bsz - 1, d), F32)], axis=0)
    mod = _modulation(cond, ada_w, ada_b).reshape(depth, MOD_ROWS, N_MOD, 1, d)

    def lat_row(tm):
        return lambda i: i // (length // tm)

    ctx_row = lambda i: bsz

    def ffn(zz, layer, which, tm, row, **kw):
        return _ffn(zz, mod, gains, ffn_w_gate, ffn_w_up, ffn_w_down, layer=layer, which=which,
                    row_of_tile=row, tm=tm, tf=FFN_SLICE, **kw)

    tm = _pick_tile(length, 512)
    tm_ffn = _pick_tile(length, 1024)
    tm_ctx = _pick_tile(t_ctx, 1024)

    pos = _pos_embed_2d(length, d, x.dtype)
    xf = x.reshape(t_lat, d)
    cf = ctx.reshape(t_ctx, d)

    h = ffn(xf, 0, 0, tm_ffn, lat_row(tm_ffn), pos=pos)
    hc = ffn(cf, 0, 0, tm_ctx, ctx_row)

    p, zg = _inproj(h, mod, gains, w_main, wz, layer=0, row_of_tile=lat_row(tm), tm=tm,
                    tn=_pick_tile(2 * dk_all, 2048), col0=0, n_out=n_main, name="gla_inproj")
    pc, zgc = _inproj(hc, mod, gains, w_main, wz, layer=0, row_of_tile=ctx_row, tm=_pick_tile(t_ctx, 512),
                      tn=_pick_tile(dk_all, 1024), col0=v_col, n_out=n_main - v_col, name="gla_inproj_ctx")

    zeros = jnp.zeros((bsz, GLA_HEADS, dv, dk), F32)
    tb_ctx = _pick_tile(ctx_len, GLA_BLOCK)
    sf, sb = _gla_scan(pc.reshape(bsz, ctx_len, -1), zgc.reshape(bsz, ctx_len, -1),
                       wdf, bdf, wdb, bdb, zeros, zeros, with_out=False,
                       q_col=q_col - v_col, k_col=k_col - v_col, v_col=0,
                       dk=dk, dv=dv, tb=tb_ctx, heads=GLA_HEADS, name="gla_scan_ctx")
    tb = _pick_tile(length, GLA_BLOCK)
    of, ob = _gla_scan(p.reshape(bsz, length, -1), zg.reshape(bsz, length, -1),
                       wdf, bdf, wdb, bdb, sf, sb, with_out=True, q_col=q_col, k_col=k_col, v_col=v_col,
                       dk=dk, dv=dv, tb=tb, heads=GLA_HEADS, name="gla_scan")
    tm_o = _pick_tile(length, 512)
    h = _gla_out(of.reshape(t_lat, dv_all), ob.reshape(t_lat, dv_all), p, og, w_out, h, mod,
                 layer=0, row_of_tile=lat_row(tm_o), tm=tm_o, dv=dv)
    h = ffn(h, 0, 1, tm_ffn, lat_row(tm_ffn))

    h = ffn(h, 1, 0, tm_ffn, lat_row(tm_ffn))
    z = _pw1(h, mod, gains, w_pw1, b_pw1, layer=1, row_of_tile=lat_row(tm), tm=tm, tn=_pick_tile(d, 1024))
    h = _conv(z, conv_w_dw[0], conv_b_dw[0].reshape(1, d), conv_ln_g[0].reshape(1, d),
              conv_ln_b[0].reshape(1, d), w_pw2, conv_b_pw2[0].reshape(1, d), h, mod,
              layer=1, bsz=bsz, tt=_pick_tile(length, 256))
    h = ffn(h, 1, 1, tm_ffn, lat_row(tm_ffn), final_gain=final_norm_g.reshape(1, d))
    return h.reshape(bsz, length, d)
```

```python
import functools

import jax
import jax.numpy as jnp
from jax import lax
from jax.experimental import pallas as pl
from jax.experimental.pallas import tpu as pltpu

F32 = jnp.float32
BF16 = jnp.bfloat16

EPS = 1e-6
N_MOD = 9
GRID_W = 64
GLA_HEADS = 4
GLA_GATE_TEMP = 16.0
LOG2_E = 1.4426950408889634
GLA_BLOCK = 256
GLA_CHUNK = 64
GLA_SUB = 16
GATE_COPY = 32
FFN_SLICE = 256
SUBLANES = 8
CONV_ROW_HALO = 16
MOD_ROWS = 8
V7X_VMEM_LIMIT = 56 * 1024 * 1024
LANE = 128


def _cparams(n_axes):
    return pltpu.CompilerParams(
        dimension_semantics=("arbitrary",) * n_axes,
        vmem_limit_bytes=V7X_VMEM_LIMIT)


def _dot(a, b):
    return jnp.dot(a, b, preferred_element_type=F32)


def _dot_tn(a, b):
    return lax.dot_general(a, b, (((0,), (0,)), ((), ())), preferred_element_type=F32)


def _silu(x):
    return x * jax.nn.sigmoid(x)


def _adanorm(z, g, shift, scale):
    y = z * lax.rsqrt(jnp.mean(z * z, axis=-1, keepdims=True) + EPS)
    return y * (g * (1.0 + scale)) + shift


def _mod_kernel(c_ref, w_ref, b_ref, o_ref):
    s = _silu(c_ref[...]).astype(BF16)
    o_ref[...] = _dot(s, w_ref[...].astype(BF16)) + b_ref[...]


def _modulation(cond, ada_w, ada_b):
    depth, d, n = ada_w.shape
    tn = _pick_tile(n, 1024)
    return pl.pallas_call(
        _mod_kernel,
        out_shape=jax.ShapeDtypeStruct((depth, MOD_ROWS, n), F32),
        grid=(depth, n // tn),
        in_specs=[
            pl.BlockSpec((MOD_ROWS, d), lambda l, j: (0, 0)),
            pl.BlockSpec((None, d, tn), lambda l, j: (l, 0, j)),
            pl.BlockSpec((None, 1, tn), lambda l, j: (l, 0, j)),
        ],
        out_specs=pl.BlockSpec((None, MOD_ROWS, tn), lambda l, j: (l, 0, j)),
        compiler_params=_cparams(2),
        name="ada_modulation",
    )(cond, ada_w, ada_b.reshape(depth, 1, n))


def _mod_spec(layer, k, row_of_tile, d):
    return pl.BlockSpec((None, None, None, 1, d),
                        lambda i, *_: (layer, row_of_tile(i), k, 0, 0))


def _gain_spec(layer, k, d):
    return pl.BlockSpec((None, None, 1, d), lambda *_: (layer, k, 0, 0))


def _ffn_kernel(*refs, has_pos, final_norm, has_tail):
    it = iter(refs)
    z_ref = next(it)
    pos_ref = next(it) if has_pos else None
    shift_ref, scale_ref, gate_ref, g_ref = next(it), next(it), next(it), next(it)
    fg_ref = next(it) if final_norm else None
    wg_ref, wu_ref, wd_ref = next(it), next(it), next(it)
    if has_tail:
        wg_tail, wu_tail, wd_tail = next(it), next(it), next(it)
    o_ref = next(it)
    u_scr = next(it)

    j = pl.program_id(1)

    def residual_in():
        return z_ref[...] + pos_ref[...] if has_pos else z_ref[...]

    def swiglu_down(u, wg, wu, wd):
        a = _dot(u, wg[...].astype(BF16))
        b = _dot(u, wu[...].astype(BF16))
        return _dot((_silu(a) * b).astype(BF16), wd[...].astype(BF16))

    @pl.when(j == 0)
    def _():
        u = _adanorm(residual_in(), g_ref[...], shift_ref[...], scale_ref[...]).astype(BF16)
        u_scr[...] = u
        o_ref[...] = swiglu_down(u, wg_tail, wu_tail, wd_tail) if has_tail else jnp.zeros_like(o_ref)

    o_ref[...] += swiglu_down(u_scr[...], wg_ref, wu_ref, wd_ref)

    @pl.when(j == pl.num_programs(1) - 1)
    def _():
        out = residual_in() + (0.5 * gate_ref[...]) * o_ref[...]
        if final_norm:
            out = out * lax.rsqrt(jnp.mean(out * out, axis=-1, keepdims=True) + EPS) * fg_ref[...]
        o_ref[...] = out


def _ffn(z, mod, gains, wg, wu, wd, *, layer, which, row_of_tile, tm, tf,
         pos=None, final_gain=None):
    t, d = z.shape
    f = wg.shape[-1]
    tail = f % tf
    n_main = f // tf
    has_tail = tail > 0
    assert tail % LANE == 0 and (not has_tail or (n_main * tf) % tail == 0)
    base = 0 if which == 0 else 6
    gain_k = 0 if which == 0 else 2
    has_pos = pos is not None
    final_norm = final_gain is not None
    once = pl.Buffered(1)

    in_specs = [pl.BlockSpec((tm, d), lambda i, j: (i, 0), pipeline_mode=once)]
    args = [z]
    if has_pos:
        n_pos = pos.shape[0] // tm
        in_specs.append(pl.BlockSpec((tm, d), lambda i, j: (i % n_pos, 0), pipeline_mode=once))
        args.append(pos)
    for k in range(3):
        in_specs.append(_mod_spec(layer, base + k, row_of_tile, d))
        args.append(mod)
    in_specs.append(_gain_spec(layer, gain_k, d))
    args.append(gains)
    if final_norm:
        in_specs.append(pl.BlockSpec((1, d), lambda i, j: (0, 0)))
        args.append(final_gain)
    in_specs += [
        pl.BlockSpec((None, None, d, tf), lambda i, j: (layer, which, 0, j)),
        pl.BlockSpec((None, None, d, tf), lambda i, j: (layer, which, 0, j)),
        pl.BlockSpec((None, None, tf, d), lambda i, j: (layer, which, j, 0)),
    ]
    args += [wg, wu, wd]
    if has_tail:
        tail_blk = n_main * tf // tail
        in_specs += [
            pl.BlockSpec((None, None, d, tail), lambda i, j: (layer, which, 0, tail_blk), pipeline_mode=once),
            pl.BlockSpec((None, None, d, tail), lambda i, j: (layer, which, 0, tail_blk), pipeline_mode=once),
            pl.BlockSpec((None, None, tail, d), lambda i, j: (layer, which, tail_blk, 0), pipeline_mode=once),
        ]
        args += [wg, wu, wd]

    return pl.pallas_call(
        functools.partial(_ffn_kernel, has_pos=has_pos, final_norm=final_norm, has_tail=has_tail),
        out_shape=jax.ShapeDtypeStruct((t, d), F32),
        grid=(t // tm, n_main),
        in_specs=in_specs,
        out_specs=pl.BlockSpec((tm, d), lambda i, j: (i, 0), pipeline_mode=once if has_pos else None),
        scratch_shapes=[pltpu.VMEM((tm, d), BF16)],
        compiler_params=_cparams(2),
        name=f"ffn_l{layer}_{which}",
    )(*args)


def _inproj_kernel(z_ref, shift_ref, scale_ref, g_ref, w_ref, wz_ref, o_ref, oz_ref, u_scr):
    j = pl.program_id(1)

    @pl.when(j == 0)
    def _():
        u = _adanorm(z_ref[...], g_ref[...], shift_ref[...], scale_ref[...]).astype(BF16)
        u_scr[...] = u
        oz_ref[...] = _dot(u, wz_ref[...])

    o_ref[...] = _dot(u_scr[...], w_ref[...].astype(BF16)).astype(o_ref.dtype)


def _inproj(z, mod, gains, w, wz, *, layer, row_of_tile, tm, tn, src_blocks, name):
    t, d = z.shape
    n_out = len(src_blocks) * tn

    def src_block(j):
        blk = src_blocks[-1]
        for n in range(len(src_blocks) - 2, -1, -1):
            blk = jnp.where(j == n, src_blocks[n], blk)
        return blk

    return pl.pallas_call(
        _inproj_kernel,
        out_shape=(jax.ShapeDtypeStruct((t, n_out), BF16),
                   jax.ShapeDtypeStruct((t, wz.shape[1]), F32)),
        grid=(t // tm, n_out // tn),
        in_specs=[
            pl.BlockSpec((tm, d), lambda i, j: (i, 0), pipeline_mode=pl.Buffered(1)),
            _mod_spec(layer, 3, row_of_tile, d),
            _mod_spec(layer, 4, row_of_tile, d),
            _gain_spec(layer, 1, d),
            pl.BlockSpec((None, d, tn), lambda i, j: (0, 0, src_block(j))),
            pl.BlockSpec(wz.shape, lambda i, j: (0, 0)),
        ],
        out_specs=(pl.BlockSpec((tm, tn), lambda i, j: (i, j)),
                   pl.BlockSpec((tm, wz.shape[1]), lambda i, j: (i, 0))),
        scratch_shapes=[pltpu.VMEM((tm, d), BF16)],
        compiler_params=_cparams(2),
        name=name,
    )(z, mod, mod, gains, w, wz)


def _log_sigmoid(x):
    return jnp.minimum(x, 0.0) - jnp.log(1.0 + jnp.exp(-jnp.abs(x)))


def _split_bf16(x, parts):
    out = []
    for _ in range(parts - 1):
        hi = x.astype(BF16)
        out.append(hi)
        x = x - hi.astype(F32)
    out.append(x.astype(BF16))
    return out


def _gate_features(z):
    lane = lax.broadcasted_iota(jnp.int32, z.shape, 1)
    is_lo = (lane >= GATE_COPY) & (lane < 2 * GATE_COPY)
    return jnp.where(is_lo, z - z.astype(BF16).astype(F32), z).astype(BF16)


def _gate_log_decay(zcat, w, bias):
    return _log_sigmoid(_dot(zcat, w) + bias) * (LOG2_E / GLA_GATE_TEMP)


def _time_tri(t, reverse):
    row = lax.broadcasted_iota(jnp.int32, (t, t), 0)
    col = lax.broadcasted_iota(jnp.int32, (t, t), 1)
    return jnp.where((col >= row) if reverse else (col <= row), 1.0, 0.0).astype(BF16)


def _cumsum_rows(tri, g):
    g_hi, g_lo = _split_bf16(g, 2)
    return _dot(tri, g_hi) + _dot(tri, g_lo)


def _score_masks(t, reverse):
    row = lax.broadcasted_iota(jnp.int32, (GLA_SUB, t), 0)
    col = lax.broadcasted_iota(jnp.int32, (GLA_SUB, t), 1)
    col_sub, col_chunk = col // GLA_SUB, col // GLA_CHUNK
    within = col - col_sub * GLA_SUB
    causal = (within >= row) if reverse else (within <= row)
    return col_sub, col_chunk, causal


def _gla_block(q, k, v, b_ref, st_ref, masks, *, reverse):
    t = b_ref.shape[0]
    n_sub, n_chunk, sub_per_chunk = t // GLA_SUB, t // GLA_CHUNK, GLA_CHUNK // GLA_SUB

    def rows(x, size, i):
        return x[i * size:(i + 1) * size]

    def b_rows(size, i):
        return b_ref[i * size:(i + 1) * size, :]

    def far_edge(size, j):
        r = j * size if reverse else (j + 1) * size - 1
        return b_ref[r:r + 1, :]

    b_end = far_edge(t, 0)
    e_chunk = [far_edge(GLA_CHUNK, j) for j in range(n_chunk)]
    k_chunk = jnp.concatenate(
        [rows(k, GLA_CHUNK, j) * jnp.exp2(e_chunk[j] - b_rows(GLA_CHUNK, j)) for j in range(n_chunk)], axis=0)

    o = None
    if q is not None:
        o = _dot((q * jnp.exp2(b_ref[...])).astype(BF16), st_ref[...].astype(BF16))
        e_sub = [far_edge(GLA_SUB, j) for j in range(n_sub)]
        k_sub = jnp.concatenate(
            [rows(k, GLA_SUB, j) * jnp.exp2(e_sub[j] - b_rows(GLA_SUB, j)) for j in range(n_sub)], axis=0)

        def visible(i, n):
            return range(i, n) if reverse else range(i + 1)

        near = [(i, j) for i in range(n_sub) for j in visible(i, n_sub)
                if j // sub_per_chunk == i // sub_per_chunk]
        far = [(i, j) for i in range(n_chunk) for j in visible(i, n_chunk) if j != i]
        lhs_near = jnp.concatenate(
            [rows(q, GLA_SUB, i) * jnp.exp2(b_rows(GLA_SUB, i) - e_sub[j]) for i, j in near], axis=0)
        lhs_far = jnp.concatenate(
            [rows(q, GLA_CHUNK, i) * jnp.exp2(b_rows(GLA_CHUNK, i) - e_chunk[j]) for i, j in far], axis=0)
        r_near = _dot(lhs_near.astype(BF16), k_sub.astype(BF16).T)
        r_far = _dot(lhs_far.astype(BF16), k_chunk.astype(BF16).T)

        col_sub, col_chunk, causal = masks
        a_rows = []
        for i in range(n_sub):
            acc = jnp.zeros((GLA_SUB, t), F32)
            for n, (pi, pj) in enumerate(near):
                if pi == i:
                    sel = (col_sub == pj) & causal if pj == i else col_sub == pj
                    acc = jnp.where(sel, rows(r_near, GLA_SUB, n), acc)
            for n, (pi, pj) in enumerate(far):
                if pi == i // sub_per_chunk:
                    piece = rows(rows(r_far, GLA_CHUNK, n), GLA_SUB, i % sub_per_chunk)
                    acc = jnp.where(col_chunk == pj, piece, acc)
            a_rows.append(acc)
        o = o + _dot(jnp.concatenate(a_rows, axis=0).astype(BF16), v)

    kd = jnp.concatenate(
        [rows(k_chunk, GLA_CHUNK, j) * jnp.exp2(b_end - e_chunk[j]) for j in range(n_chunk)], axis=0)
    decay = jnp.exp2(jnp.broadcast_to(b_end, (LANE, b_end.shape[1])).T)
    st_ref[...] = st_ref[...] * jnp.tile(decay, (1, st_ref.shape[1] // LANE)) + _dot_tn(kd.astype(BF16), v)
    return o


def _gla_scan_kernel(*refs, with_out, q_scale, heads, dk, dv):
    it = iter(refs)
    if with_out:
        qf_ref, qb_ref = next(it), next(it)
    kf_ref, kb_ref, vf_ref, vb_ref, zf_ref, zb_ref = (next(it) for _ in range(6))
    wdf_ref, bdf_ref, wdb_ref, bdb_ref, s0f_ref, s0b_ref = (next(it) for _ in range(6))
    if with_out:
        of_ref, ob_ref = next(it), next(it)
    else:
        sf_out, sb_out = next(it), next(it)
    sf_scr, sb_scr, bf_scr, bb_scr = next(it), next(it), next(it), next(it)

    step = pl.program_id(2)
    tb = zf_ref.shape[0]

    @pl.when(step == 0)
    def _():
        sf_scr[...] = s0f_ref[...]
        sb_scr[...] = s0b_ref[...]

    def direction(q_ref, k_ref, v_ref, z_ref, wd_ref, bd_ref, b_scr, s_scr, o_ref, reverse):
        zcat = _gate_features(z_ref[...])
        tri = _time_tri(tb, reverse)
        masks = _score_masks(tb, reverse) if with_out else None
        for h in range(heads):
            ck, cv = slice(h * dk, (h + 1) * dk), slice(h * dv, (h + 1) * dv)
            b_scr[h] = _cumsum_rows(tri, _gate_log_decay(zcat, wd_ref[:, ck], bd_ref[:, ck]))
            o = _gla_block(q_ref[:, ck].astype(F32) * q_scale if with_out else None,
                           k_ref[:, ck].astype(F32), v_ref[:, cv], b_scr.at[h], s_scr.at[h], masks,
                           reverse=reverse)
            if with_out:
                o_ref[:, cv] = o.astype(o_ref.dtype)

    direction(qf_ref if with_out else None, kf_ref, vf_ref, zf_ref, wdf_ref, bdf_ref, bf_scr, sf_scr,
              of_ref if with_out else None, False)
    direction(qb_ref if with_out else None, kb_ref, vb_ref, zb_ref, wdb_ref, bdb_ref, bb_scr, sb_scr,
              ob_ref if with_out else None, True)

    if not with_out:
        @pl.when(step == pl.num_programs(2) - 1)
        def _():
            sf_out[...] = sf_scr[...]
            sb_out[...] = sb_scr[...]


def _gla_scan(p, zg, wdf, bdf, wdb, bdb, s0f, s0b, *, with_out, q_col, k_col, v_col,
              dk, dv, tb, heads, name):
    bsz, length, _ = p.shape
    nh = GLA_HEADS
    n = length // tb
    zw = zg.shape[-1]
    wk, wv = heads * dk, heads * dv
    assert nh % heads == 0 and q_col % wk == 0 and k_col % wk == 0 and v_col % wv == 0
    kq0, kk0, kv0 = q_col // wk, k_col // wk, v_col // wv

    def fwd(c):
        return c

    def bwd(c):
        return n - 1 - c

    def tile(width, c0, order):
        return pl.BlockSpec((None, tb, width), lambda b, g, c: (b, order(c), c0 + g))

    state = pl.BlockSpec((None, heads, dk, dv), lambda b, g, c: (b, g, 0, 0))
    in_specs, args = [], []
    if with_out:
        in_specs += [tile(wk, kq0, fwd), tile(wk, kq0, bwd)]
        args += [p, p]
    in_specs += [tile(wk, kk0, fwd), tile(wk, kk0, bwd),
                 tile(wv, kv0, fwd), tile(wv, kv0, bwd),
                 pl.BlockSpec((None, tb, zw), lambda b, g, c: (b, c, 0)),
                 pl.BlockSpec((None, tb, zw), lambda b, g, c: (b, n - 1 - c, 0)),
                 pl.BlockSpec((zw, wk), lambda b, g, c: (0, g)),
                 pl.BlockSpec((1, wk), lambda b, g, c: (0, g)),
                 pl.BlockSpec((zw, wk), lambda b, g, c: (0, g)),
                 pl.BlockSpec((1, wk), lambda b, g, c: (0, g)),
                 state, state]
    args += [p, p, p, p, zg, zg, wdf, bdf, wdb, bdb, s0f, s0b]

    if with_out:
        out_shape = (jax.ShapeDtypeStruct((bsz, length, nh * dv), BF16),) * 2
        out_specs = (pl.BlockSpec((None, tb, wv), lambda b, g, c: (b, c, g)),
                     pl.BlockSpec((None, tb, wv), lambda b, g, c: (b, n - 1 - c, g)))
    else:
        out_shape = (jax.ShapeDtypeStruct((bsz, nh, dk, dv), F32),) * 2
        out_specs = (state, state)

    return pl.pallas_call(
        functools.partial(_gla_scan_kernel, with_out=with_out, q_scale=dk ** -0.5,
                          heads=heads, dk=dk, dv=dv),
        out_shape=out_shape,
        grid=(bsz, nh // heads, n),
        in_specs=in_specs,
        out_specs=out_specs,
        scratch_shapes=[pltpu.VMEM((heads, dk, dv), F32), pltpu.VMEM((heads, dk, dv), F32),
                        pltpu.VMEM((heads, tb, dk), F32), pltpu.VMEM((heads, tb, dk), F32)],
        compiler_params=_cparams(3),
        name=name,
    )(*args)


def _gla_out_kernel(of_ref, ob_ref, r_ref, og_ref, w_ref, h_ref, gate_ref, o_ref, *, dv):
    o = of_ref[...].astype(F32) + ob_ref[...].astype(F32)
    heads = []
    for hh in range(GLA_HEADS):
        oh = o[:, hh * dv:(hh + 1) * dv]
        heads.append(oh * lax.rsqrt(jnp.mean(oh * oh, axis=-1, keepdims=True) + EPS))
    o = jnp.concatenate(heads, axis=-1) * og_ref[...]
    y = (o * _silu(r_ref[...].astype(F32))).astype(BF16)
    o_ref[...] = h_ref[...] + gate_ref[...] * _dot(y, w_ref[...])


def _gla_out(of, ob, p, og, w_out, h, mod, *, layer, row_of_tile, tm, dv):
    t, d = h.shape
    n_in = w_out.shape[0]
    return pl.pallas_call(
        functools.partial(_gla_out_kernel, dv=dv),
        out_shape=jax.ShapeDtypeStruct((t, d), F32),
        grid=(t // tm,),
        in_specs=[
            pl.BlockSpec((tm, n_in), lambda i: (i, 0)),
            pl.BlockSpec((tm, n_in), lambda i: (i, 0)),
            pl.BlockSpec((tm, n_in), lambda i: (i, 0)),
            pl.BlockSpec((1, n_in), lambda i: (0, 0)),
            pl.BlockSpec((n_in, d), lambda i: (0, 0), pipeline_mode=pl.Buffered(1)),
            pl.BlockSpec((tm, d), lambda i: (i, 0)),
            _mod_spec(layer, 5, row_of_tile, d),
        ],
        out_specs=pl.BlockSpec((tm, d), lambda i: (i, 0)),
        compiler_params=_cparams(1),
        name="gla_out",
    )(of, ob, p, og, w_out, h, mod)


def _pw1_kernel(z_ref, shift_ref, scale_ref, g_ref, wa_ref, wg_ref, ba_ref, bg_ref, o_ref, u_scr):
    j = pl.program_id(1)

    @pl.when(j == 0)
    def _():
        u_scr[...] = _adanorm(z_ref[...], g_ref[...], shift_ref[...], scale_ref[...]).astype(BF16)

    u = u_scr[...]
    a = _dot(u, wa_ref[...].astype(BF16)) + ba_ref[...]
    gt = _dot(u, wg_ref[...].astype(BF16)) + bg_ref[...]
    o_ref[...] = a * jax.nn.sigmoid(gt)


def _pw1(z, mod, gains, w, b, *, layer, row_of_tile, tm, tn):
    t, d = z.shape
    n_half = w.shape[1] // 2
    nb = n_half // tn
    return pl.pallas_call(
        _pw1_kernel,
        out_shape=jax.ShapeDtypeStruct((t, n_half), F32),
        grid=(t // tm, nb),
        in_specs=[
            pl.BlockSpec((tm, d), lambda i, j: (i, 0)),
            _mod_spec(layer, 3, row_of_tile, d),
            _mod_spec(layer, 4, row_of_tile, d),
            _gain_spec(layer, 1, d),
            pl.BlockSpec((d, tn), lambda i, j: (0, j)),
            pl.BlockSpec((d, tn), lambda i, j: (0, nb + j)),
            pl.BlockSpec((1, tn), lambda i, j: (0, j)),
            pl.BlockSpec((1, tn), lambda i, j: (0, nb + j)),
        ],
        out_specs=pl.BlockSpec((tm, tn), lambda i, j: (i, j)),
        scratch_shapes=[pltpu.VMEM((tm, d), BF16)],
        compiler_params=_cparams(2),
        name="conv_pw1_glu",
    )(z, mod, mod, gains, w, w, b, b)


def _conv_kernel(zp_ref, zc_ref, zn_ref, wdw_ref, bdw_ref, lng_ref, lnb_ref, w2_ref, b2_ref,
                 h_ref, gate_ref, o_ref, ext_scr, dw_scr, *, width, rows_per_pass):
    i = pl.program_id(1)
    tt, d = zc_ref.shape
    halo = CONV_ROW_HALO
    pad = width // 2

    ext_scr[0:halo, :] = jnp.where(i > 0, zp_ref[...], 0.0)
    ext_scr[halo:halo + tt, :] = zc_ref[...]
    ext_scr[halo + tt:, :] = jnp.where(i < pl.num_programs(1) - 1, zn_ref[...], 0.0)

    first = halo - pad

    def strip(cs, carry):
        cols = pl.ds(pl.multiple_of(cs * LANE, LANE), LANE)
        for rb in range(tt // rows_per_pass):
            r0 = rb * rows_per_pass
            acc = jnp.zeros((rows_per_pass, LANE), F32)
            for res in range(SUBLANES):
                part = None
                for s in range(first, first + width):
                    if s % SUBLANES != res:
                        continue
                    src = r0 + s - res
                    term = (ext_scr[src:src + rows_per_pass + SUBLANES, cols]
                            * wdw_ref[s - first:s - first + 1, cols])
                    part = term if part is None else part + term
                acc = acc + part[res:res + rows_per_pass]
            dw_scr[r0:r0 + rows_per_pass, cols] = acc
        return carry

    lax.fori_loop(0, d // LANE, strip, 0)

    zc = dw_scr[...] + bdw_ref[...]
    mu = jnp.mean(zc, axis=-1, keepdims=True)
    zc = zc - mu
    var = jnp.mean(zc * zc, axis=-1, keepdims=True)
    zn = zc * lax.rsqrt(var + EPS) * lng_ref[...] + lnb_ref[...]
    y = _dot(_silu(zn).astype(BF16), w2_ref[...]) + b2_ref[...]
    o_ref[...] = h_ref[...] + gate_ref[...] * y


def _conv(z, wdw, bdw, lng, lnb, w2, b2, h, mod, *, layer, bsz, tt):
    t, d = h.shape
    length = t // bsz
    nt = length // tt
    width = wdw.shape[0]
    hb = tt // CONV_ROW_HALO
    n_halo = length // CONV_ROW_HALO
    z3 = z.reshape(bsz, length, d)
    vec = pl.BlockSpec((1, d), lambda b, i: (0, 0))
    return pl.pallas_call(
        functools.partial(_conv_kernel, width=width, rows_per_pass=64),
        out_shape=jax.ShapeDtypeStruct((bsz, length, d), F32),
        grid=(bsz, nt),
        in_specs=[
            pl.BlockSpec((None, CONV_ROW_HALO, d), lambda b, i: (b, jnp.maximum(i * hb - 1, 0), 0)),
            pl.BlockSpec((None, tt, d), lambda b, i: (b, i, 0)),
            pl.BlockSpec((None, CONV_ROW_HALO, d),
                         lambda b, i: (b, jnp.minimum((i + 1) * hb, n_halo - 1), 0)),
            pl.BlockSpec((width, d), lambda b, i: (0, 0)),
            vec, vec, vec,
            pl.BlockSpec((d, d), lambda b, i: (0, 0), pipeline_mode=pl.Buffered(1)),
            vec,
            pl.BlockSpec((None, tt, d), lambda b, i: (b, i, 0)),
            pl.BlockSpec((None, None, None, 1, d), lambda b, i: (layer, b, 5, 0, 0)),
        ],
        out_specs=pl.BlockSpec((None, tt, d), lambda b, i: (b, i, 0)),
        scratch_shapes=[pltpu.VMEM((tt + 2 * CONV_ROW_HALO, d), F32), pltpu.VMEM((tt, d), F32)],
        compiler_params=_cparams(2),
        name="conv_dw_ln_pw2",
    )(z3, z3, z3, wdw, bdw, lng, lnb, w2, b2, h.reshape(bsz, length, d), mod).reshape(t, d)


def _pos_embed_2d(n_tok, d, dtype):
    rows = n_tok // GRID_W
    quarter = d // 4
    omega = 1.0 / (10000.0 ** (jnp.arange(quarter, dtype=F32) / quarter))
    ar = jnp.arange(rows, dtype=F32)[:, None] * omega[None]
    ac = jnp.arange(GRID_W, dtype=F32)[:, None] * omega[None]
    row_part = jnp.concatenate([jnp.sin(ar), jnp.cos(ar)], axis=-1)[:, None, :]
    col_part = jnp.concatenate([jnp.sin(ac), jnp.cos(ac)], axis=-1)[None, :, :]
    shape = (rows, GRID_W, 2 * quarter)
    emb = jnp.concatenate([jnp.broadcast_to(row_part, shape), jnp.broadcast_to(col_part, shape)], axis=-1)
    return emb.reshape(n_tok, d).astype(dtype)


def _pick_tile(n, pref):
    t = min(pref, n)
    while n % t:
        t //= 2
    return t


def kernel(x, c, ctx, c_ctx, ada_w, ada_b, norm_g, final_norm_g, ffn_w_gate, ffn_w_up, ffn_w_down,
           gla_w_in, gla_w_decay_f, gla_b_decay_f, gla_w_decay_b, gla_b_decay_b, gla_out_norm_g,
           gla_w_out, conv_w_pw1, conv_b_pw1, conv_w_dw, conv_b_dw, conv_ln_g, conv_ln_b,
           conv_w_pw2, conv_b_pw2):
    bsz, length, d = x.shape
    ctx_len = ctx.shape[1]
    depth = ada_w.shape[0]
    assert depth == 2 and bsz + 1 <= MOD_ROWS
    f = ffn_w_gate.shape[-1]
    dk_all, dv_all = d // 2, d
    dk, dv = dk_all // GLA_HEADS, dv_all // GLA_HEADS
    rank = gla_w_decay_f.shape[1]
    t_lat, t_ctx = bsz * length, bsz * ctx_len

    gains = norm_g.reshape(depth, 3, 1, d)

    w_in = gla_w_in
    qr = dk_all + dv_all
    assert dv_all == 2 * dk_all
    q_blk, r_blk, k_blk, v_blk = [0], [1, 2], [3], [4, 5]
    n_main = qr + dk_all + dv_all
    v_col, q_col, k_col = dv_all, 2 * dv_all, 2 * dv_all + dk_all
    assert 2 * rank == GATE_COPY
    wz1 = gla_w_in[0, :, n_main:].astype(BF16)
    wz = jnp.concatenate([wz1, wz1, wz1, jnp.zeros((d, LANE - 3 * GATE_COPY), BF16)], axis=1)

    def gate_weight(w, first_row):
        hi = w.astype(BF16)
        lo = (w - hi.astype(F32)).astype(BF16)
        out = jnp.zeros((LANE, w.shape[1]), BF16)
        for copy, part in enumerate((hi, hi, lo)):
            out = lax.dynamic_update_slice(out, part, (copy * GATE_COPY + first_row, 0))
        return out

    wdf = gate_weight(gla_w_decay_f[0], 0)
    wdb = gate_weight(gla_w_decay_b[0], rank)
    bdf = gla_b_decay_f[0].reshape(1, dk_all)
    bdb = gla_b_decay_b[0].reshape(1, dk_all)
    og = gla_out_norm_g[0].reshape(1, dv_all)
    w_out = gla_w_out[0].astype(BF16)

    w_pw1 = conv_w_pw1[0]
    b_pw1 = conv_b_pw1[0].reshape(1, -1)
    w_pw2 = conv_w_pw2[0].astype(BF16)

    cond = jnp.concatenate([c, c_ctx[None], jnp.zeros((MOD_ROWS - bsz - 1, d), F32)], axis=0)
    mod = _modulation(cond, ada_w, ada_b).reshape(depth, MOD_ROWS, N_MOD, 1, d)

    def lat_row(tm):
        return lambda i: i // (length // tm)

    ctx_row = lambda i: bsz

    def ffn(zz, layer, which, tm, row, **kw):
        return _ffn(zz, mod, gains, ffn_w_gate, ffn_w_up, ffn_w_down, layer=layer, which=which,
                    row_of_tile=row, tm=tm, tf=FFN_SLICE, **kw)

    tm_ffn = _pick_tile(length, 1024)
    tm_ctx = _pick_tile(t_ctx, 1024)

    pos = _pos_embed_2d(length, d, x.dtype)
    xf = x.reshape(t_lat, d)
    cf = ctx.reshape(t_ctx, d)

    h = ffn(xf, 0, 0, tm_ffn, lat_row(tm_ffn), pos=pos)
    hc = ffn(cf, 0, 0, tm_ctx, ctx_row)

    p, zg = _inproj(h, mod, gains, w_in, wz, layer=0, row_of_tile=lat_row(tm_ffn), tm=tm_ffn, tn=dk_all,
                    src_blocks=r_blk + v_blk + q_blk + k_blk, name="gla_inproj")
    pc, zgc = _inproj(hc, mod, gains, w_in, wz, layer=0, row_of_tile=ctx_row, tm=_pick_tile(t_ctx, 512),
                      tn=dk_all, src_blocks=v_blk + k_blk, name="gla_inproj_ctx")

    zeros = jnp.zeros((bsz, GLA_HEADS, dk, dv), F32)
    tb_ctx = _pick_tile(ctx_len, GLA_BLOCK)
    sf, sb = _gla_scan(pc.reshape(bsz, ctx_len, -1), zgc.reshape(bsz, ctx_len, -1),
                       wdf, bdf, wdb, bdb, zeros, zeros, with_out=False,
                       q_col=0, k_col=dv_all, v_col=0,
                       dk=dk, dv=dv, tb=tb_ctx, heads=GLA_HEADS, name="gla_scan_ctx")
    tb = _pick_tile(length, GLA_BLOCK)
    of, ob = _gla_scan(p.reshape(bsz, length, -1), zg.reshape(bsz, length, -1),
                       wdf, bdf, wdb, bdb, sf, sb, with_out=True, q_col=q_col, k_col=k_col, v_col=v_col,
                       dk=dk, dv=dv, tb=tb, heads=GLA_HEADS, name="gla_scan")
    tm_o = _pick_tile(length, 512)
    h = _gla_out(of.reshape(t_lat, dv_all), ob.reshape(t_lat, dv_all), p, og, w_out, h, mod,
                 layer=0, row_of_tile=lat_row(tm_o), tm=tm_o, dv=dv)
    h = ffn(h, 0, 1, tm_ffn, lat_row(tm_ffn))

    h = ffn(h, 1, 0, tm_ffn, lat_row(tm_ffn))
    z = _pw1(h, mod, gains, w_pw1, b_pw1, layer=1, row_of_tile=lat_row(tm_ffn), tm=tm_ffn,
             tn=_pick_tile(d, 512))
    h = _conv(z, conv_w_dw[0], conv_b_dw[0].reshape(1, d), conv_ln_g[0].reshape(1, d),
              conv_ln_b[0].reshape(1, d), w_pw2, conv_b_pw2[0].reshape(1, d), h, mod,
              layer=1, bsz=bsz, tt=_pick_tile(length, 256))
    h = ffn(h, 1, 1, tm_ffn, lat_row(tm_ffn), final_gain=final_norm_g.reshape(1, d))
    return h.reshape(bsz, length, d)
```

```python
import functools

import jax
import jax.numpy as jnp
from jax import lax
from jax.experimental import pallas as pl
from jax.experimental.pallas import tpu as pltpu

F32 = jnp.float32
BF16 = jnp.bfloat16

EPS = 1e-6
N_MOD = 9
GRID_W = 64
GLA_HEADS = 4
GLA_GATE_TEMP = 16.0
LOG2_E = 1.4426950408889634
GLA_BLOCK = 256
GLA_CHUNK = 64
GLA_SUB = 16
GATE_COPY = 32
FFN_SLICE = 256
SUBLANES = 8
CONV_ROW_HALO = 16
MOD_ROWS = 8
V7X_VMEM_LIMIT = 56 * 1024 * 1024
LANE = 128


def _cparams(n_axes):
    return pltpu.CompilerParams(
        dimension_semantics=("arbitrary",) * n_axes,
        vmem_limit_bytes=V7X_VMEM_LIMIT)


def _dot(a, b):
    return jnp.dot(a, b, preferred_element_type=F32)


def _dot_tn(a, b):
    return lax.dot_general(a, b, (((0,), (0,)), ((), ())), preferred_element_type=F32)


def _silu(x):
    return x * jax.nn.sigmoid(x)


def _adanorm(z, g, shift, scale):
    y = z * lax.rsqrt(jnp.mean(z * z, axis=-1, keepdims=True) + EPS)
    return y * (g * (1.0 + scale)) + shift


def _mod_kernel(c_ref, w_ref, b_ref, o_ref):
    s = _silu(c_ref[...]).astype(BF16)
    o_ref[...] = _dot(s, w_ref[...].astype(BF16)) + b_ref[...]


def _modulation(cond, ada_w, ada_b):
    depth, d, n = ada_w.shape
    tn = _pick_tile(n, 1024)
    return pl.pallas_call(
        _mod_kernel,
        out_shape=jax.ShapeDtypeStruct((depth, MOD_ROWS, n), F32),
        grid=(depth, n // tn),
        in_specs=[
            pl.BlockSpec((MOD_ROWS, d), lambda l, j: (0, 0)),
            pl.BlockSpec((None, d, tn), lambda l, j: (l, 0, j)),
            pl.BlockSpec((None, 1, tn), lambda l, j: (l, 0, j)),
        ],
        out_specs=pl.BlockSpec((None, MOD_ROWS, tn), lambda l, j: (l, 0, j)),
        compiler_params=_cparams(2),
        name="ada_modulation",
    )(cond, ada_w, ada_b.reshape(depth, 1, n))


def _mod_spec(layer, k, row_of_tile, d):
    return pl.BlockSpec((None, None, None, 1, d),
                        lambda i, *_: (layer, row_of_tile(i), k, 0, 0))


def _gain_spec(layer, k, d):
    return pl.BlockSpec((None, None, 1, d), lambda *_: (layer, k, 0, 0))


def _ffn_kernel(*refs, has_pos, final_norm, has_tail):
    it = iter(refs)
    z_ref = next(it)
    pos_ref = next(it) if has_pos else None
    shift_ref, scale_ref, gate_ref, g_ref = next(it), next(it), next(it), next(it)
    fg_ref = next(it) if final_norm else None
    wg_ref, wu_ref, wd_ref = next(it), next(it), next(it)
    if has_tail:
        wg_tail, wu_tail, wd_tail = next(it), next(it), next(it)
    o_ref = next(it)
    u_scr = next(it)

    j = pl.program_id(1)

    def residual_in():
        return z_ref[...] + pos_ref[...] if has_pos else z_ref[...]

    def swiglu_down(u, wg, wu, wd):
        a = _dot(u, wg[...].astype(BF16))
        b = _dot(u, wu[...].astype(BF16))
        return _dot((_silu(a) * b).astype(BF16), wd[...].astype(BF16))

    @pl.when(j == 0)
    def _():
        u = _adanorm(residual_in(), g_ref[...], shift_ref[...], scale_ref[...]).astype(BF16)
        u_scr[...] = u
        o_ref[...] = swiglu_down(u, wg_tail, wu_tail, wd_tail) if has_tail else jnp.zeros_like(o_ref)

    o_ref[...] += swiglu_down(u_scr[...], wg_ref, wu_ref, wd_ref)

    @pl.when(j == pl.num_programs(1) - 1)
    def _():
        out = residual_in() + (0.5 * gate_ref[...]) * o_ref[...]
        if final_norm:
            out = out * lax.rsqrt(jnp.mean(out * out, axis=-1, keepdims=True) + EPS) * fg_ref[...]
        o_ref[...] = out


def _ffn(z, mod, gains, wg, wu, wd, *, layer, which, row_of_tile, tm, tf,
         pos=None, final_gain=None):
    t, d = z.shape
    f = wg.shape[-1]
    tail = f % tf
    n_main = f // tf
    has_tail = tail > 0
    assert tail % LANE == 0 and (not has_tail or (n_main * tf) % tail == 0)
    base = 0 if which == 0 else 6
    gain_k = 0 if which == 0 else 2
    has_pos = pos is not None
    final_norm = final_gain is not None
    once = pl.Buffered(1)

    in_specs = [pl.BlockSpec((tm, d), lambda i, j: (i, 0), pipeline_mode=once)]
    args = [z]
    if has_pos:
        n_pos = pos.shape[0] // tm
        in_specs.append(pl.BlockSpec((tm, d), lambda i, j: (i % n_pos, 0), pipeline_mode=once))
        args.append(pos)
    for k in range(3):
        in_specs.append(_mod_spec(layer, base + k, row_of_tile, d))
        args.append(mod)
    in_specs.append(_gain_spec(layer, gain_k, d))
    args.append(gains)
    if final_norm:
        in_specs.append(pl.BlockSpec((1, d), lambda i, j: (0, 0)))
        args.append(final_gain)
    in_specs += [
        pl.BlockSpec((None, None, d, tf), lambda i, j: (layer, which, 0, j)),
        pl.BlockSpec((None, None, d, tf), lambda i, j: (layer, which, 0, j)),
        pl.BlockSpec((None, None, tf, d), lambda i, j: (layer, which, j, 0)),
    ]
    args += [wg, wu, wd]
    if has_tail:
        tail_blk = n_main * tf // tail
        in_specs += [
            pl.BlockSpec((None, None, d, tail), lambda i, j: (layer, which, 0, tail_blk), pipeline_mode=once),
            pl.BlockSpec((None, None, d, tail), lambda i, j: (layer, which, 0, tail_blk), pipeline_mode=once),
            pl.BlockSpec((None, None, tail, d), lambda i, j: (layer, which, tail_blk, 0), pipeline_mode=once),
        ]
        args += [wg, wu, wd]

    return pl.pallas_call(
        functools.partial(_ffn_kernel, has_pos=has_pos, final_norm=final_norm, has_tail=has_tail),
        out_shape=jax.ShapeDtypeStruct((t, d), F32),
        grid=(t // tm, n_main),
        in_specs=in_specs,
        out_specs=pl.BlockSpec((tm, d), lambda i, j: (i, 0), pipeline_mode=once if has_pos else None),
        scratch_shapes=[pltpu.VMEM((tm, d), BF16)],
        compiler_params=_cparams(2),
        name=f"ffn_l{layer}_{which}",
    )(*args)


def _inproj_kernel(z_ref, shift_ref, scale_ref, g_ref, w_ref, wz_ref, o_ref, oz_ref, u_scr):
    j = pl.program_id(1)

    @pl.when(j == 0)
    def _():
        u = _adanorm(z_ref[...], g_ref[...], shift_ref[...], scale_ref[...]).astype(BF16)
        u_scr[...] = u
        oz_ref[...] = _dot(u, wz_ref[...])

    o_ref[...] = lax.dot_general(u_scr[...], w_ref[...].astype(BF16), (((1,), (1,)), ((), ())),
                                 preferred_element_type=F32).astype(o_ref.dtype)


def _inproj(z, mod, gains, w, wz, *, layer, row_of_tile, tm, tn, src_blocks, name):
    t, d = z.shape
    n_out = len(src_blocks) * tn

    def src_block(j):
        blk = src_blocks[-1]
        for n in range(len(src_blocks) - 2, -1, -1):
            blk = jnp.where(j == n, src_blocks[n], blk)
        return blk

    return pl.pallas_call(
        _inproj_kernel,
        out_shape=(jax.ShapeDtypeStruct((t, n_out), BF16),
                   jax.ShapeDtypeStruct((t, wz.shape[1]), F32)),
        grid=(t // tm, n_out // tn),
        in_specs=[
            pl.BlockSpec((tm, d), lambda i, j: (i, 0), pipeline_mode=pl.Buffered(1)),
            _mod_spec(layer, 3, row_of_tile, d),
            _mod_spec(layer, 4, row_of_tile, d),
            _gain_spec(layer, 1, d),
            pl.BlockSpec((None, tn, d), lambda i, j: (0, src_block(j), 0)),
            pl.BlockSpec(wz.shape, lambda i, j: (0, 0)),
        ],
        out_specs=(pl.BlockSpec((tm, tn), lambda i, j: (i, j)),
                   pl.BlockSpec((tm, wz.shape[1]), lambda i, j: (i, 0))),
        scratch_shapes=[pltpu.VMEM((tm, d), BF16)],
        compiler_params=_cparams(2),
        name=name,
    )(z, mod, mod, gains, w, wz)


def _log_sigmoid(x):
    return jnp.minimum(x, 0.0) - jnp.log(1.0 + jnp.exp(-jnp.abs(x)))


def _split_bf16(x, parts):
    out = []
    for _ in range(parts - 1):
        hi = x.astype(BF16)
        out.append(hi)
        x = x - hi.astype(F32)
    out.append(x.astype(BF16))
    return out


def _gate_features(z):
    lane = lax.broadcasted_iota(jnp.int32, z.shape, 1)
    is_lo = (lane >= GATE_COPY) & (lane < 2 * GATE_COPY)
    return jnp.where(is_lo, z - z.astype(BF16).astype(F32), z).astype(BF16)


def _gate_log_decay(zcat, w, bias):
    return _log_sigmoid(_dot(zcat, w) + bias) * (LOG2_E / GLA_GATE_TEMP)


def _time_tri(t, reverse):
    row = lax.broadcasted_iota(jnp.int32, (t, t), 0)
    col = lax.broadcasted_iota(jnp.int32, (t, t), 1)
    return jnp.where((col >= row) if reverse else (col <= row), 1.0, 0.0).astype(BF16)


def _cumsum_rows(tri, g):
    g_hi, g_lo = _split_bf16(g, 2)
    return _dot(tri, g_hi) + _dot(tri, g_lo)


def _score_masks(t, reverse):
    row = lax.broadcasted_iota(jnp.int32, (GLA_SUB, t), 0)
    col = lax.broadcasted_iota(jnp.int32, (GLA_SUB, t), 1)
    col_sub, col_chunk = col // GLA_SUB, col // GLA_CHUNK
    within = col - col_sub * GLA_SUB
    causal = (within >= row) if reverse else (within <= row)
    return col_sub, col_chunk, causal


def _gla_block(q, k, v, b_ref, st_ref, masks, *, reverse):
    t = b_ref.shape[0]
    n_sub, n_chunk, sub_per_chunk = t // GLA_SUB, t // GLA_CHUNK, GLA_CHUNK // GLA_SUB

    def rows(x, size, i):
        return x[i * size:(i + 1) * size]

    def b_rows(size, i):
        return b_ref[i * size:(i + 1) * size, :]

    def far_edge(size, j):
        r = j * size if reverse else (j + 1) * size - 1
        return b_ref[r:r + 1, :]

    b_end = far_edge(t, 0)
    e_chunk = [far_edge(GLA_CHUNK, j) for j in range(n_chunk)]
    k_chunk = jnp.concatenate(
        [rows(k, GLA_CHUNK, j) * jnp.exp2(e_chunk[j] - b_rows(GLA_CHUNK, j)) for j in range(n_chunk)], axis=0)

    o = None
    if q is not None:
        o = _dot((q * jnp.exp2(b_ref[...])).astype(BF16), st_ref[...].astype(BF16))
        e_sub = [far_edge(GLA_SUB, j) for j in range(n_sub)]
        k_sub = jnp.concatenate(
            [rows(k, GLA_SUB, j) * jnp.exp2(e_sub[j] - b_rows(GLA_SUB, j)) for j in range(n_sub)], axis=0)

        def visible(i, n):
            return range(i, n) if reverse else range(i + 1)

        near = [(i, j) for i in range(n_sub) for j in visible(i, n_sub)
                if j // sub_per_chunk == i // sub_per_chunk]
        far = [(i, j) for i in range(n_chunk) for j in visible(i, n_chunk) if j != i]
        lhs_near = jnp.concatenate(
            [rows(q, GLA_SUB, i) * jnp.exp2(b_rows(GLA_SUB, i) - e_sub[j]) for i, j in near], axis=0)
        lhs_far = jnp.concatenate(
            [rows(q, GLA_CHUNK, i) * jnp.exp2(b_rows(GLA_CHUNK, i) - e_chunk[j]) for i, j in far], axis=0)
        r_near = _dot(lhs_near.astype(BF16), k_sub.astype(BF16).T)
        r_far = _dot(lhs_far.astype(BF16), k_chunk.astype(BF16).T)

        col_sub, col_chunk, causal = masks
        a_rows = []
        for i in range(n_sub):
            acc = jnp.zeros((GLA_SUB, t), F32)
            for n, (pi, pj) in enumerate(near):
                if pi == i:
                    sel = (col_sub == pj) & causal if pj == i else col_sub == pj
                    acc = jnp.where(sel, rows(r_near, GLA_SUB, n), acc)
            for n, (pi, pj) in enumerate(far):
                if pi == i // sub_per_chunk:
                    piece = rows(rows(r_far, GLA_CHUNK, n), GLA_SUB, i % sub_per_chunk)
                    acc = jnp.where(col_chunk == pj, piece, acc)
            a_rows.append(acc)
        o = o + _dot(jnp.concatenate(a_rows, axis=0).astype(BF16), v)

    kd = jnp.concatenate(
        [rows(k_chunk, GLA_CHUNK, j) * jnp.exp2(b_end - e_chunk[j]) for j in range(n_chunk)], axis=0)
    decay = jnp.exp2(jnp.broadcast_to(b_end, (LANE, b_end.shape[1])).T)
    st_ref[...] = st_ref[...] * jnp.tile(decay, (1, st_ref.shape[1] // LANE)) + _dot_tn(kd.astype(BF16), v)
    return o


def _gla_scan_kernel(*refs, with_out, q_scale, heads, dk, dv):
    it = iter(refs)
    if with_out:
        qf_ref, qb_ref = next(it), next(it)
    kf_ref, kb_ref, vf_ref, vb_ref, zf_ref, zb_ref = (next(it) for _ in range(6))
    wdf_ref, bdf_ref, wdb_ref, bdb_ref, s0f_ref, s0b_ref = (next(it) for _ in range(6))
    if with_out:
        of_ref, ob_ref = next(it), next(it)
    else:
        sf_out, sb_out = next(it), next(it)
    sf_scr, sb_scr, bf_scr, bb_scr = next(it), next(it), next(it), next(it)

    step = pl.program_id(2)
    tb = zf_ref.shape[0]

    @pl.when(step == 0)
    def _():
        sf_scr[...] = s0f_ref[...]
        sb_scr[...] = s0b_ref[...]

    def direction(q_ref, k_ref, v_ref, z_ref, wd_ref, bd_ref, b_scr, s_scr, o_ref, reverse):
        zcat = _gate_features(z_ref[...])
        tri = _time_tri(tb, reverse)
        masks = _score_masks(tb, reverse) if with_out else None
        for h in range(heads):
            ck, cv = slice(h * dk, (h + 1) * dk), slice(h * dv, (h + 1) * dv)
            b_scr[h] = _cumsum_rows(tri, _gate_log_decay(zcat, wd_ref[:, ck], bd_ref[:, ck]))
            o = _gla_block(q_ref[:, ck].astype(F32) * q_scale if with_out else None,
                           k_ref[:, ck].astype(F32), v_ref[:, cv], b_scr.at[h], s_scr.at[h], masks,
                           reverse=reverse)
            if with_out:
                o_ref[:, cv] = o.astype(o_ref.dtype)

    direction(qf_ref if with_out else None, kf_ref, vf_ref, zf_ref, wdf_ref, bdf_ref, bf_scr, sf_scr,
              of_ref if with_out else None, False)
    direction(qb_ref if with_out else None, kb_ref, vb_ref, zb_ref, wdb_ref, bdb_ref, bb_scr, sb_scr,
              ob_ref if with_out else None, True)

    if not with_out:
        @pl.when(step == pl.num_programs(2) - 1)
        def _():
            sf_out[...] = sf_scr[...]
            sb_out[...] = sb_scr[...]


def _gla_scan(p, zg, wdf, bdf, wdb, bdb, s0f, s0b, *, with_out, q_col, k_col, v_col,
              dk, dv, tb, heads, name):
    bsz, length, _ = p.shape
    nh = GLA_HEADS
    n = length // tb
    zw = zg.shape[-1]
    wk, wv = heads * dk, heads * dv
    assert nh % heads == 0 and q_col % wk == 0 and k_col % wk == 0 and v_col % wv == 0
    kq0, kk0, kv0 = q_col // wk, k_col // wk, v_col // wv

    def fwd(c):
        return c

    def bwd(c):
        return n - 1 - c

    def tile(width, c0, order):
        return pl.BlockSpec((None, tb, width), lambda b, g, c: (b, order(c), c0 + g))

    state = pl.BlockSpec((None, heads, dk, dv), lambda b, g, c: (b, g, 0, 0))
    in_specs, args = [], []
    if with_out:
        in_specs += [tile(wk, kq0, fwd), tile(wk, kq0, bwd)]
        args += [p, p]
    in_specs += [tile(wk, kk0, fwd), tile(wk, kk0, bwd),
                 tile(wv, kv0, fwd), tile(wv, kv0, bwd),
                 pl.BlockSpec((None, tb, zw), lambda b, g, c: (b, c, 0)),
                 pl.BlockSpec((None, tb, zw), lambda b, g, c: (b, n - 1 - c, 0)),
                 pl.BlockSpec((zw, wk), lambda b, g, c: (0, g)),
                 pl.BlockSpec((1, wk), lambda b, g, c: (0, g)),
                 pl.BlockSpec((zw, wk), lambda b, g, c: (0, g)),
                 pl.BlockSpec((1, wk), lambda b, g, c: (0, g)),
                 state, state]
    args += [p, p, p, p, zg, zg, wdf, bdf, wdb, bdb, s0f, s0b]

    if with_out:
        out_shape = (jax.ShapeDtypeStruct((bsz, length, nh * dv), BF16),) * 2
        out_specs = (pl.BlockSpec((None, tb, wv), lambda b, g, c: (b, c, g)),
                     pl.BlockSpec((None, tb, wv), lambda b, g, c: (b, n - 1 - c, g)))
    else:
        out_shape = (jax.ShapeDtypeStruct((bsz, nh, dk, dv), F32),) * 2
        out_specs = (state, state)

    return pl.pallas_call(
        functools.partial(_gla_scan_kernel, with_out=with_out, q_scale=dk ** -0.5,
                          heads=heads, dk=dk, dv=dv),
        out_shape=out_shape,
        grid=(bsz, nh // heads, n),
        in_specs=in_specs,
        out_specs=out_specs,
        scratch_shapes=[pltpu.VMEM((heads, dk, dv), F32), pltpu.VMEM((heads, dk, dv), F32),
                        pltpu.VMEM((heads, tb, dk), F32), pltpu.VMEM((heads, tb, dk), F32)],
        compiler_params=_cparams(3),
        name=name,
    )(*args)


def _gla_out_kernel(of_ref, ob_ref, r_ref, og_ref, w_ref, h_ref, gate_ref, o_ref, *, dv):
    o = of_ref[...].astype(F32) + ob_ref[...].astype(F32)
    heads = []
    for hh in range(GLA_HEADS):
        oh = o[:, hh * dv:(hh + 1) * dv]
        heads.append(oh * lax.rsqrt(jnp.mean(oh * oh, axis=-1, keepdims=True) + EPS))
    o = jnp.concatenate(heads, axis=-1) * og_ref[...]
    y = (o * _silu(r_ref[...].astype(F32))).astype(BF16)
    o_ref[...] = h_ref[...] + gate_ref[...] * _dot(y, w_ref[...])


def _gla_out(of, ob, p, og, w_out, h, mod, *, layer, row_of_tile, tm, dv):
    t, d = h.shape
    n_in = w_out.shape[0]
    return pl.pallas_call(
        functools.partial(_gla_out_kernel, dv=dv),
        out_shape=jax.ShapeDtypeStruct((t, d), F32),
        grid=(t // tm,),
        in_specs=[
            pl.BlockSpec((tm, n_in), lambda i: (i, 0)),
            pl.BlockSpec((tm, n_in), lambda i: (i, 0)),
            pl.BlockSpec((tm, n_in), lambda i: (i, 0)),
            pl.BlockSpec((1, n_in), lambda i: (0, 0)),
            pl.BlockSpec((n_in, d), lambda i: (0, 0), pipeline_mode=pl.Buffered(1)),
            pl.BlockSpec((tm, d), lambda i: (i, 0)),
            _mod_spec(layer, 5, row_of_tile, d),
        ],
        out_specs=pl.BlockSpec((tm, d), lambda i: (i, 0)),
        compiler_params=_cparams(1),
        name="gla_out",
    )(of, ob, p, og, w_out, h, mod)


def _pw1_kernel(z_ref, shift_ref, scale_ref, g_ref, wa_ref, wg_ref, ba_ref, bg_ref, o_ref, u_scr):
    j = pl.program_id(1)

    @pl.when(j == 0)
    def _():
        u_scr[...] = _adanorm(z_ref[...], g_ref[...], shift_ref[...], scale_ref[...]).astype(BF16)

    u = u_scr[...]
    a = _dot(u, wa_ref[...].astype(BF16)) + ba_ref[...]
    gt = _dot(u, wg_ref[...].astype(BF16)) + bg_ref[...]
    o_ref[...] = a * jax.nn.sigmoid(gt)


def _pw1(z, mod, gains, w, b, *, layer, row_of_tile, tm, tn):
    t, d = z.shape
    n_half = w.shape[1] // 2
    nb = n_half // tn
    return pl.pallas_call(
        _pw1_kernel,
        out_shape=jax.ShapeDtypeStruct((t, n_half), F32),
        grid=(t // tm, nb),
        in_specs=[
            pl.BlockSpec((tm, d), lambda i, j: (i, 0)),
            _mod_spec(layer, 3, row_of_tile, d),
            _mod_spec(layer, 4, row_of_tile, d),
            _gain_spec(layer, 1, d),
            pl.BlockSpec((d, tn), lambda i, j: (0, j)),
            pl.BlockSpec((d, tn), lambda i, j: (0, nb + j)),
            pl.BlockSpec((1, tn), lambda i, j: (0, j)),
            pl.BlockSpec((1, tn), lambda i, j: (0, nb + j)),
        ],
        out_specs=pl.BlockSpec((tm, tn), lambda i, j: (i, j)),
        scratch_shapes=[pltpu.VMEM((tm, d), BF16)],
        compiler_params=_cparams(2),
        name="conv_pw1_glu",
    )(z, mod, mod, gains, w, w, b, b)


def _conv_kernel(zp_ref, zc_ref, zn_ref, wdw_ref, bdw_ref, lng_ref, lnb_ref, w2_ref, b2_ref,
                 h_ref, gate_ref, o_ref, ext_scr, dw_scr, *, width, rows_per_pass):
    i = pl.program_id(1)
    tt, d = zc_ref.shape
    halo = CONV_ROW_HALO
    pad = width // 2

    ext_scr[0:halo, :] = jnp.where(i > 0, zp_ref[...], 0.0)
    ext_scr[halo:halo + tt, :] = zc_ref[...]
    ext_scr[halo + tt:, :] = jnp.where(i < pl.num_programs(1) - 1, zn_ref[...], 0.0)

    first = halo - pad

    def strip(cs, carry):
        cols = pl.ds(pl.multiple_of(cs * LANE, LANE), LANE)
        for rb in range(tt // rows_per_pass):
            r0 = rb * rows_per_pass
            acc = jnp.zeros((rows_per_pass, LANE), F32)
            for res in range(SUBLANES):
                part = None
                for s in range(first, first + width):
                    if s % SUBLANES != res:
                        continue
                    src = r0 + s - res
                    term = (ext_scr[src:src + rows_per_pass + SUBLANES, cols]
                            * wdw_ref[s - first:s - first + 1, cols])
                    part = term if part is None else part + term
                acc = acc + part[res:res + rows_per_pass]
            dw_scr[r0:r0 + rows_per_pass, cols] = acc
        return carry

    lax.fori_loop(0, d // LANE, strip, 0)

    zc = dw_scr[...] + bdw_ref[...]
    mu = jnp.mean(zc, axis=-1, keepdims=True)
    zc = zc - mu
    var = jnp.mean(zc * zc, axis=-1, keepdims=True)
    zn = zc * lax.rsqrt(var + EPS) * lng_ref[...] + lnb_ref[...]
    y = _dot(_silu(zn).astype(BF16), w2_ref[...]) + b2_ref[...]
    o_ref[...] = h_ref[...] + gate_ref[...] * y


def _conv(z, wdw, bdw, lng, lnb, w2, b2, h, mod, *, layer, bsz, tt):
    t, d = h.shape
    length = t // bsz
    nt = length // tt
    width = wdw.shape[0]
    hb = tt // CONV_ROW_HALO
    n_halo = length // CONV_ROW_HALO
    z3 = z.reshape(bsz, length, d)
    vec = pl.BlockSpec((1, d), lambda b, i: (0, 0))
    return pl.pallas_call(
        functools.partial(_conv_kernel, width=width, rows_per_pass=128),
        out_shape=jax.ShapeDtypeStruct((bsz, length, d), F32),
        grid=(bsz, nt),
        in_specs=[
            pl.BlockSpec((None, CONV_ROW_HALO, d), lambda b, i: (b, jnp.maximum(i * hb - 1, 0), 0)),
            pl.BlockSpec((None, tt, d), lambda b, i: (b, i, 0)),
            pl.BlockSpec((None, CONV_ROW_HALO, d),
                         lambda b, i: (b, jnp.minimum((i + 1) * hb, n_halo - 1), 0)),
            pl.BlockSpec((width, d), lambda b, i: (0, 0)),
            vec, vec, vec,
            pl.BlockSpec((d, d), lambda b, i: (0, 0), pipeline_mode=pl.Buffered(1)),
            vec,
            pl.BlockSpec((None, tt, d), lambda b, i: (b, i, 0)),
            pl.BlockSpec((None, None, None, 1, d), lambda b, i: (layer, b, 5, 0, 0)),
        ],
        out_specs=pl.BlockSpec((None, tt, d), lambda b, i: (b, i, 0)),
        scratch_shapes=[pltpu.VMEM((tt + 2 * CONV_ROW_HALO, d), F32), pltpu.VMEM((tt, d), F32)],
        compiler_params=_cparams(2),
        name="conv_dw_ln_pw2",
    )(z3, z3, z3, wdw, bdw, lng, lnb, w2, b2, h.reshape(bsz, length, d), mod).reshape(t, d)


def _pos_embed_2d(n_tok, d, dtype):
    rows = n_tok // GRID_W
    quarter = d // 4
    omega = 1.0 / (10000.0 ** (jnp.arange(quarter, dtype=F32) / quarter))
    ar = jnp.arange(rows, dtype=F32)[:, None] * omega[None]
    ac = jnp.arange(GRID_W, dtype=F32)[:, None] * omega[None]
    row_part = jnp.concatenate([jnp.sin(ar), jnp.cos(ar)], axis=-1)[:, None, :]
    col_part = jnp.concatenate([jnp.sin(ac), jnp.cos(ac)], axis=-1)[None, :, :]
    shape = (rows, GRID_W, 2 * quarter)
    emb = jnp.concatenate([jnp.broadcast_to(row_part, shape), jnp.broadcast_to(col_part, shape)], axis=-1)
    return emb.reshape(n_tok, d).astype(dtype)


def _pick_tile(n, pref):
    t = min(pref, n)
    while n % t:
        t //= 2
    return t


def kernel(x, c, ctx, c_ctx, ada_w, ada_b, norm_g, final_norm_g, ffn_w_gate, ffn_w_up, ffn_w_down,
           gla_w_in, gla_w_decay_f, gla_b_decay_f, gla_w_decay_b, gla_b_decay_b, gla_out_norm_g,
           gla_w_out, conv_w_pw1, conv_b_pw1, conv_w_dw, conv_b_dw, conv_ln_g, conv_ln_b,
           conv_w_pw2, conv_b_pw2):
    bsz, length, d = x.shape
    ctx_len = ctx.shape[1]
    depth = ada_w.shape[0]
    assert depth == 2 and bsz + 1 <= MOD_ROWS
    f = ffn_w_gate.shape[-1]
    dk_all, dv_all = d // 2, d
    dk, dv = dk_all // GLA_HEADS, dv_all // GLA_HEADS
    rank = gla_w_decay_f.shape[1]
    t_lat, t_ctx = bsz * length, bsz * ctx_len

    gains = norm_g.reshape(depth, 3, 1, d)

    w_in = jnp.transpose(gla_w_in, (0, 2, 1))
    qr = dk_all + dv_all
    assert dv_all == 2 * dk_all
    q_blk, r_blk, k_blk, v_blk = [0], [1, 2], [3], [4, 5]
    n_main = qr + dk_all + dv_all
    v_col, q_col, k_col = dv_all, 2 * dv_all, 2 * dv_all + dk_all
    assert 2 * rank == GATE_COPY
    wz1 = gla_w_in[0, :, n_main:].astype(BF16)
    wz = jnp.concatenate([wz1, wz1, wz1, jnp.zeros((d, LANE - 3 * GATE_COPY), BF16)], axis=1)

    def gate_weight(w, first_row):
        hi = w.astype(BF16)
        lo = (w - hi.astype(F32)).astype(BF16)
        out = jnp.zeros((LANE, w.shape[1]), BF16)
        for copy, part in enumerate((hi, hi, lo)):
            out = lax.dynamic_update_slice(out, part, (copy * GATE_COPY + first_row, 0))
        return out

    wdf = gate_weight(gla_w_decay_f[0], 0)
    wdb = gate_weight(gla_w_decay_b[0], rank)
    bdf = gla_b_decay_f[0].reshape(1, dk_all)
    bdb = gla_b_decay_b[0].reshape(1, dk_all)
    og = gla_out_norm_g[0].reshape(1, dv_all)
    w_out = gla_w_out[0].astype(BF16)

    w_pw1 = conv_w_pw1[0]
    b_pw1 = conv_b_pw1[0].reshape(1, -1)
    w_pw2 = conv_w_pw2[0].astype(BF16)

    cond = jnp.concatenate([c, c_ctx[None], jnp.zeros((MOD_ROWS - bsz - 1, d), F32)], axis=0)
    mod = _modulation(cond, ada_w, ada_b).reshape(depth, MOD_ROWS, N_MOD, 1, d)

    def lat_row(tm):
        return lambda i: i // (length // tm)

    ctx_row = lambda i: bsz

    def ffn(zz, layer, which, tm, row, **kw):
        return _ffn(zz, mod, gains, ffn_w_gate, ffn_w_up, ffn_w_down, layer=layer, which=which,
                    row_of_tile=row, tm=tm, tf=FFN_SLICE, **kw)

    tm_ffn = _pick_tile(length, 1024)
    tm_ctx = _pick_tile(t_ctx, 1024)

    pos = _pos_embed_2d(length, d, x.dtype)
    xf = x.reshape(t_lat, d)
    cf = ctx.reshape(t_ctx, d)

    h = ffn(xf, 0, 0, tm_ffn, lat_row(tm_ffn), pos=pos)
    hc = ffn(cf, 0, 0, tm_ctx, ctx_row)

    p, zg = _inproj(h, mod, gains, w_in, wz, layer=0, row_of_tile=lat_row(tm_ffn), tm=tm_ffn, tn=dk_all,
                    src_blocks=r_blk + v_blk + q_blk + k_blk, name="gla_inproj")
    pc, zgc = _inproj(hc, mod, gains, w_in, wz, layer=0, row_of_tile=ctx_row, tm=_pick_tile(t_ctx, 512),
                      tn=dk_all, src_blocks=v_blk + k_blk, name="gla_inproj_ctx")

    zeros = jnp.zeros((bsz, GLA_HEADS, dk, dv), F32)
    tb_ctx = _pick_tile(ctx_len, GLA_BLOCK)
    sf, sb = _gla_scan(pc.reshape(bsz, ctx_len, -1), zgc.reshape(bsz, ctx_len, -1),
                       wdf, bdf, wdb, bdb, zeros, zeros, with_out=False,
                       q_col=0, k_col=dv_all, v_col=0,
                       dk=dk, dv=dv, tb=tb_ctx, heads=GLA_HEADS, name="gla_scan_ctx")
    tb = _pick_tile(length, GLA_BLOCK)
    of, ob = _gla_scan(p.reshape(bsz, length, -1), zg.reshape(bsz, length, -1),
                       wdf, bdf, wdb, bdb, sf, sb, with_out=True, q_col=q_col, k_col=k_col, v_col=v_col,
                       dk=dk, dv=dv, tb=tb, heads=GLA_HEADS // 2, name="gla_scan")
    tm_o = _pick_tile(length, 512)
    h = _gla_out(of.reshape(t_lat, dv_all), ob.reshape(t_lat, dv_all), p, og, w_out, h, mod,
                 layer=0, row_of_tile=lat_row(tm_o), tm=tm_o, dv=dv)
    h = ffn(h, 0, 1, tm_ffn, lat_row(tm_ffn))

    h = ffn(h, 1, 0, tm_ffn, lat_row(tm_ffn))
    z = _pw1(h, mod, gains, w_pw1, b_pw1, layer=1, row_of_tile=lat_row(tm_ffn), tm=tm_ffn,
             tn=_pick_tile(d, 512))
    h = _conv(z, conv_w_dw[0], conv_b_dw[0].reshape(1, d), conv_ln_g[0].reshape(1, d),
              conv_ln_b[0].reshape(1, d), w_pw2, conv_b_pw2[0].reshape(1, d), h, mod,
              layer=1, bsz=bsz, tt=_pick_tile(length, 256))
    h = ffn(h, 1, 1, tm_ffn, lat_row(tm_ffn), final_gain=final_norm_g.reshape(1, d))
    return h.reshape(bsz, length, d)
```

```python
import functools

import jax
import jax.numpy as jnp
from jax import lax
from jax.experimental import pallas as pl
from jax.experimental.pallas import tpu as pltpu

F32 = jnp.float32
BF16 = jnp.bfloat16

EPS = 1e-6
N_MOD = 9
GRID_W = 64
GLA_HEADS = 4
GLA_GATE_TEMP = 16.0
LOG2_E = 1.4426950408889634
GLA_BLOCK = 256
GLA_CHUNK = 64
GLA_SUB = 16
GATE_COPY = 32
FFN_SLICE = 256
SUBLANES = 8
CONV_ROW_HALO = 16
MOD_ROWS = 8
V7X_VMEM_LIMIT = 60 * 1024 * 1024
LANE = 128


def _cparams(n_axes, flags=None):
    return pltpu.CompilerParams(
        dimension_semantics=("arbitrary",) * n_axes,
        vmem_limit_bytes=V7X_VMEM_LIMIT,
        flags=flags)


def _dot(a, b):
    return jnp.dot(a, b, preferred_element_type=F32)


def _dot_tn(a, b):
    return lax.dot_general(a, b, (((0,), (0,)), ((), ())), preferred_element_type=F32)


def _silu(x):
    return x * jax.nn.sigmoid(x)


def _adanorm(z, g, shift, scale):
    y = z * lax.rsqrt(jnp.mean(z * z, axis=-1, keepdims=True) + EPS)
    return y * (g * (1.0 + scale)) + shift


def _mod_kernel(c_ref, w_ref, b_ref, o_ref):
    s = _silu(c_ref[...]).astype(BF16)
    o_ref[...] = _dot(s, w_ref[...].astype(BF16)) + b_ref[...]


def _modulation(cond, ada_w, ada_b):
    depth, d, n = ada_w.shape
    tn = _pick_tile(n, 1024)
    return pl.pallas_call(
        _mod_kernel,
        out_shape=jax.ShapeDtypeStruct((depth, MOD_ROWS, n), F32),
        grid=(depth, n // tn),
        in_specs=[
            pl.BlockSpec((MOD_ROWS, d), lambda l, j: (0, 0)),
            pl.BlockSpec((None, d, tn), lambda l, j: (l, 0, j)),
            pl.BlockSpec((None, 1, tn), lambda l, j: (l, 0, j)),
        ],
        out_specs=pl.BlockSpec((None, MOD_ROWS, tn), lambda l, j: (l, 0, j)),
        compiler_params=_cparams(2),
        name="ada_modulation",
    )(cond, ada_w, ada_b.reshape(depth, 1, n))


def _mod_spec(layer, k, row_of_tile, d):
    return pl.BlockSpec((None, None, None, 1, d),
                        lambda i, *_: (layer, row_of_tile(i), k, 0, 0))


def _gain_spec(layer, k, d):
    return pl.BlockSpec((None, None, 1, d), lambda *_: (layer, k, 0, 0))


def _ffn_kernel(*refs, has_pos, final_norm, has_tail):
    it = iter(refs)
    z_ref = next(it)
    if has_pos:
        prow_ref, pcol_ref = next(it), next(it)
    shift_ref, scale_ref, gate_ref, g_ref = next(it), next(it), next(it), next(it)
    fg_ref = next(it) if final_norm else None
    wg_ref, wu_ref, wd_ref = next(it), next(it), next(it)
    if has_tail:
        wg_tail, wu_tail, wd_tail = next(it), next(it), next(it)
    o_ref = next(it)
    u_scr = next(it)

    j = pl.program_id(1)

    def residual_in():
        z = z_ref[...]
        if has_pos:
            grid_w, half = pcol_ref.shape
            cols = pcol_ref[...]
            z = z + jnp.concatenate(
                [jnp.concatenate([jnp.broadcast_to(prow_ref[r:r + 1, :], (grid_w, half)), cols], axis=1)
                 for r in range(prow_ref.shape[0])], axis=0)
        return z

    def swiglu_down(u, wg, wu, wd):
        a = _dot(u, wg[...].astype(BF16))
        b = _dot(u, wu[...].astype(BF16))
        return _dot((_silu(a) * b).astype(BF16), wd[...].astype(BF16))

    @pl.when(j == 0)
    def _():
        u = _adanorm(residual_in(), g_ref[...], shift_ref[...], scale_ref[...]).astype(BF16)
        u_scr[...] = u
        o_ref[...] = swiglu_down(u, wg_tail, wu_tail, wd_tail) if has_tail else jnp.zeros_like(o_ref)

    o_ref[...] += swiglu_down(u_scr[...], wg_ref, wu_ref, wd_ref)

    @pl.when(j == pl.num_programs(1) - 1)
    def _():
        out = residual_in() + (0.5 * gate_ref[...]) * o_ref[...]
        if final_norm:
            out = out * lax.rsqrt(jnp.mean(out * out, axis=-1, keepdims=True) + EPS) * fg_ref[...]
        o_ref[...] = out


def _ffn(z, mod, gains, wg, wu, wd, *, layer, which, row_of_tile, tm, tf,
         pos=None, final_gain=None):
    t, d = z.shape
    f = wg.shape[-1]
    tail = f % tf
    n_main = f // tf
    has_tail = tail > 0
    assert tail % LANE == 0 and (not has_tail or (n_main * tf) % tail == 0)
    base = 0 if which == 0 else 6
    gain_k = 0 if which == 0 else 2
    has_pos = pos is not None
    final_norm = final_gain is not None
    once = pl.Buffered(1)

    in_specs = [pl.BlockSpec((tm, d), lambda i, j: (i, 0))]
    args = [z]
    if has_pos:
        row_tab, col_tab = pos
        rows_per_tile = tm // col_tab.shape[0]
        n_pos = row_tab.shape[0] // rows_per_tile
        in_specs += [pl.BlockSpec((rows_per_tile, row_tab.shape[1]), lambda i, j: (i % n_pos, 0)),
                     pl.BlockSpec(col_tab.shape, lambda i, j: (0, 0))]
        args += [row_tab, col_tab]
    for k in range(3):
        in_specs.append(_mod_spec(layer, base + k, row_of_tile, d))
        args.append(mod)
    in_specs.append(_gain_spec(layer, gain_k, d))
    args.append(gains)
    if final_norm:
        in_specs.append(pl.BlockSpec((1, d), lambda i, j: (0, 0)))
        args.append(final_gain)
    in_specs += [
        pl.BlockSpec((None, None, d, tf), lambda i, j: (layer, which, 0, j)),
        pl.BlockSpec((None, None, d, tf), lambda i, j: (layer, which, 0, j)),
        pl.BlockSpec((None, None, tf, d), lambda i, j: (layer, which, j, 0)),
    ]
    args += [wg, wu, wd]
    if has_tail:
        tail_blk = n_main * tf // tail
        in_specs += [
            pl.BlockSpec((None, None, d, tail), lambda i, j: (layer, which, 0, tail_blk), pipeline_mode=once),
            pl.BlockSpec((None, None, d, tail), lambda i, j: (layer, which, 0, tail_blk), pipeline_mode=once),
            pl.BlockSpec((None, None, tail, d), lambda i, j: (layer, which, tail_blk, 0), pipeline_mode=once),
        ]
        args += [wg, wu, wd]

    return pl.pallas_call(
        functools.partial(_ffn_kernel, has_pos=has_pos, final_norm=final_norm, has_tail=has_tail),
        out_shape=jax.ShapeDtypeStruct((t, d), F32),
        grid=(t // tm, n_main),
        in_specs=in_specs,
        out_specs=pl.BlockSpec((tm, d), lambda i, j: (i, 0)),
        scratch_shapes=[pltpu.VMEM((tm, d), BF16)],
        compiler_params=_cparams(2),
        name=f"ffn_l{layer}_{which}",
    )(*args)


def _inproj_kernel(z_ref, shift_ref, scale_ref, g_ref, w_ref, wz_ref, o_ref, oz_ref, u_scr):
    j = pl.program_id(1)

    @pl.when(j == 0)
    def _():
        u = _adanorm(z_ref[...], g_ref[...], shift_ref[...], scale_ref[...]).astype(BF16)
        u_scr[...] = u
        oz_ref[...] = _dot(u, wz_ref[...])

    o_ref[...] = lax.dot_general(u_scr[...], w_ref[...].astype(BF16), (((1,), (1,)), ((), ())),
                                 preferred_element_type=F32).astype(o_ref.dtype)


def _inproj(z, mod, gains, w, wz, *, layer, row_of_tile, tm, tn, src_blocks, name):
    t, d = z.shape
    n_out = len(src_blocks) * tn

    def src_block(j):
        blk = src_blocks[-1]
        for n in range(len(src_blocks) - 2, -1, -1):
            blk = jnp.where(j == n, src_blocks[n], blk)
        return blk

    return pl.pallas_call(
        _inproj_kernel,
        out_shape=(jax.ShapeDtypeStruct((t, n_out), BF16),
                   jax.ShapeDtypeStruct((t, wz.shape[1]), F32)),
        grid=(t // tm, n_out // tn),
        in_specs=[
            pl.BlockSpec((tm, d), lambda i, j: (i, 0)),
            _mod_spec(layer, 3, row_of_tile, d),
            _mod_spec(layer, 4, row_of_tile, d),
            _gain_spec(layer, 1, d),
            pl.BlockSpec((None, tn, d), lambda i, j: (0, src_block(j), 0)),
            pl.BlockSpec(wz.shape, lambda i, j: (0, 0)),
        ],
        out_specs=(pl.BlockSpec((tm, tn), lambda i, j: (i, j)),
                   pl.BlockSpec((tm, wz.shape[1]), lambda i, j: (i, 0))),
        scratch_shapes=[pltpu.VMEM((tm, d), BF16)],
        compiler_params=_cparams(2),
        name=name,
    )(z, mod, mod, gains, w, wz)


def _log_sigmoid(x):
    return jnp.minimum(x, 0.0) - jnp.log(1.0 + jnp.exp(-jnp.abs(x)))


def _split_bf16(x, parts):
    out = []
    for _ in range(parts - 1):
        hi = x.astype(BF16)
        out.append(hi)
        x = x - hi.astype(F32)
    out.append(x.astype(BF16))
    return out


def _gate_features(z):
    lane = lax.broadcasted_iota(jnp.int32, z.shape, 1)
    is_lo = (lane >= GATE_COPY) & (lane < 2 * GATE_COPY)
    return jnp.where(is_lo, z - z.astype(BF16).astype(F32), z).astype(BF16)


def _gate_log_decay(zcat, w, bias):
    return _log_sigmoid(_dot(zcat, w) + bias) * (LOG2_E / GLA_GATE_TEMP)


def _time_tri(t, reverse):
    row = lax.broadcasted_iota(jnp.int32, (t, t), 0)
    col = lax.broadcasted_iota(jnp.int32, (t, t), 1)
    return jnp.where((col >= row) if reverse else (col <= row), 1.0, 0.0).astype(BF16)


def _cumsum_rows(tri, g):
    g_hi, g_lo = _split_bf16(g, 2)
    return _dot(tri, g_hi) + _dot(tri, g_lo)


def _score_masks(t, reverse):
    row = lax.broadcasted_iota(jnp.int32, (GLA_SUB, t), 0)
    col = lax.broadcasted_iota(jnp.int32, (GLA_SUB, t), 1)
    col_sub, col_chunk = col // GLA_SUB, col // GLA_CHUNK
    within = col - col_sub * GLA_SUB
    causal = (within >= row) if reverse else (within <= row)
    return col_sub, col_chunk, causal


def _gla_block(q, k, v, b_ref, st_ref, masks, *, reverse):
    t = b_ref.shape[0]
    n_sub, n_chunk, sub_per_chunk = t // GLA_SUB, t // GLA_CHUNK, GLA_CHUNK // GLA_SUB

    def rows(x, size, i):
        return x[i * size:(i + 1) * size]

    def b_rows(size, i):
        return b_ref[i * size:(i + 1) * size, :]

    def far_edge(size, j):
        r = j * size if reverse else (j + 1) * size - 1
        return b_ref[r:r + 1, :]

    b_end = far_edge(t, 0)
    e_chunk = [far_edge(GLA_CHUNK, j) for j in range(n_chunk)]
    k_chunk = jnp.concatenate(
        [rows(k, GLA_CHUNK, j) * jnp.exp2(e_chunk[j] - b_rows(GLA_CHUNK, j)) for j in range(n_chunk)], axis=0)
    yield

    o = None
    if q is not None:
        o = _dot((q * jnp.exp2(b_ref[...])).astype(BF16), st_ref[...].astype(BF16))
        e_sub = [far_edge(GLA_SUB, j) for j in range(n_sub)]
        k_sub = jnp.concatenate(
            [rows(k, GLA_SUB, j) * jnp.exp2(e_sub[j] - b_rows(GLA_SUB, j)) for j in range(n_sub)], axis=0)

        def visible(i, n):
            return range(i, n) if reverse else range(i + 1)

        yield
        near = [(i, j) for i in range(n_sub) for j in visible(i, n_sub)
                if j // sub_per_chunk == i // sub_per_chunk]
        far = [(i, j) for i in range(n_chunk) for j in visible(i, n_chunk) if j != i]
        lhs_near = jnp.concatenate(
            [rows(q, GLA_SUB, i) * jnp.exp2(b_rows(GLA_SUB, i) - e_sub[j]) for i, j in near], axis=0)
        lhs_far = jnp.concatenate(
            [rows(q, GLA_CHUNK, i) * jnp.exp2(b_rows(GLA_CHUNK, i) - e_chunk[j]) for i, j in far], axis=0)
        yield
        r_near = _dot(lhs_near.astype(BF16), k_sub.astype(BF16).T)
        r_far = _dot(lhs_far.astype(BF16), k_chunk.astype(BF16).T)

        yield
        col_sub, col_chunk, causal = masks
        a_rows = []
        for i in range(n_sub):
            acc = jnp.zeros((GLA_SUB, t), F32)
            for n, (pi, pj) in enumerate(near):
                if pi == i:
                    sel = (col_sub == pj) & causal if pj == i else col_sub == pj
                    acc = jnp.where(sel, rows(r_near, GLA_SUB, n), acc)
            for n, (pi, pj) in enumerate(far):
                if pi == i // sub_per_chunk:
                    piece = rows(rows(r_far, GLA_CHUNK, n), GLA_SUB, i % sub_per_chunk)
                    acc = jnp.where(col_chunk == pj, piece, acc)
            a_rows.append(acc)
        yield
        o = o + _dot(jnp.concatenate(a_rows, axis=0).astype(BF16), v)

    kd = jnp.concatenate(
        [rows(k_chunk, GLA_CHUNK, j) * jnp.exp2(b_end - e_chunk[j]) for j in range(n_chunk)], axis=0)
    decay = jnp.exp2(jnp.broadcast_to(b_end, (LANE, b_end.shape[1])).T)
    st_ref[...] = st_ref[...] * jnp.tile(decay, (1, st_ref.shape[1] // LANE)) + _dot_tn(kd.astype(BF16), v)
    return o


def _gla_scan_kernel(*refs, with_out, q_scale, heads, dk, dv):
    it = iter(refs)
    if with_out:
        qf_ref, qb_ref = next(it), next(it)
    kf_ref, kb_ref, vf_ref, vb_ref, zf_ref, zb_ref = (next(it) for _ in range(6))
    wdf_ref, bdf_ref, wdb_ref, bdb_ref, s0f_ref, s0b_ref = (next(it) for _ in range(6))
    if with_out:
        of_ref, ob_ref = next(it), next(it)
    else:
        sf_out, sb_out = next(it), next(it)
    sf_scr, sb_scr, bf_scr, bb_scr = ([next(it) for _ in range(heads)] for _ in range(4))

    step = pl.program_id(2)
    tb = zf_ref.shape[0]

    @pl.when(step == 0)
    def _():
        for h in range(heads):
            sf_scr[h][...] = s0f_ref[h]
            sb_scr[h][...] = s0b_ref[h]

    def direction(q_ref, k_ref, v_ref, z_ref, wd_ref, bd_ref, b_scr, s_scr, o_ref, reverse):
        zcat = _gate_features(z_ref[...])
        tri = _time_tri(tb, reverse)
        masks = _score_masks(tb, reverse) if with_out else None
        chains = []
        for h in range(heads):
            ck, cv = slice(h * dk, (h + 1) * dk), slice(h * dv, (h + 1) * dv)
            b_scr[h][...] = _cumsum_rows(tri, _gate_log_decay(zcat, wd_ref[:, ck], bd_ref[:, ck]))
            block = _gla_block(q_ref[:, ck].astype(F32) * q_scale if with_out else None,
                               k_ref[:, ck].astype(F32), v_ref[:, cv], b_scr[h], s_scr[h], masks,
                               reverse=reverse)
            chains.append((block, o_ref, cv))
        return chains

    live = (direction(qf_ref if with_out else None, kf_ref, vf_ref, zf_ref, wdf_ref, bdf_ref, bf_scr, sf_scr,
                      of_ref if with_out else None, False)
            + direction(qb_ref if with_out else None, kb_ref, vb_ref, zb_ref, wdb_ref, bdb_ref, bb_scr, sb_scr,
                        ob_ref if with_out else None, True))
    waiting, live = live, []
    while waiting or live:
        if waiting:
            live.append(waiting.pop(0))
        unfinished = []
        for block, o_ref, cv in live:
            try:
                next(block)
                unfinished.append((block, o_ref, cv))
            except StopIteration as done:
                if with_out:
                    o_ref[:, cv] = done.value.astype(o_ref.dtype)
        live = unfinished

    if not with_out:
        @pl.when(step == pl.num_programs(2) - 1)
        def _():
            for h in range(heads):
                sf_out[h] = sf_scr[h][...]
                sb_out[h] = sb_scr[h][...]


def _gla_scan(p, zg, wdf, bdf, wdb, bdb, s0f, s0b, *, with_out, q_col, k_col, v_col,
              dk, dv, tb, heads, name):
    bsz, length, _ = p.shape
    nh = GLA_HEADS
    n = length // tb
    zw = zg.shape[-1]
    wk, wv = heads * dk, heads * dv
    assert nh % heads == 0 and q_col % wk == 0 and k_col % wk == 0 and v_col % wv == 0
    kq0, kk0, kv0 = q_col // wk, k_col // wk, v_col // wv

    def fwd(c):
        return c

    def bwd(c):
        return n - 1 - c

    def tile(width, c0, order):
        return pl.BlockSpec((None, tb, width), lambda b, g, c: (b, order(c), c0 + g))

    state = pl.BlockSpec((None, heads, dk, dv), lambda b, g, c: (b, g, 0, 0))
    in_specs, args = [], []
    if with_out:
        in_specs += [tile(wk, kq0, fwd), tile(wk, kq0, bwd)]
        args += [p, p]
    in_specs += [tile(wk, kk0, fwd), tile(wk, kk0, bwd),
                 tile(wv, kv0, fwd), tile(wv, kv0, bwd),
                 pl.BlockSpec((None, tb, zw), lambda b, g, c: (b, c, 0)),
                 pl.BlockSpec((None, tb, zw), lambda b, g, c: (b, n - 1 - c, 0)),
                 pl.BlockSpec((zw, wk), lambda b, g, c: (0, g)),
                 pl.BlockSpec((1, wk), lambda b, g, c: (0, g)),
                 pl.BlockSpec((zw, wk), lambda b, g, c: (0, g)),
                 pl.BlockSpec((1, wk), lambda b, g, c: (0, g)),
                 state, state]
    args += [p, p, p, p, zg, zg, wdf, bdf, wdb, bdb, s0f, s0b]

    if with_out:
        out_shape = (jax.ShapeDtypeStruct((bsz, length, nh * dv), BF16),) * 2
        out_specs = (pl.BlockSpec((None, tb, wv), lambda b, g, c: (b, c, g)),
                     pl.BlockSpec((None, tb, wv), lambda b, g, c: (b, n - 1 - c, g)))
    else:
        out_shape = (jax.ShapeDtypeStruct((bsz, nh, dk, dv), F32),) * 2
        out_specs = (state, state)

    return pl.pallas_call(
        functools.partial(_gla_scan_kernel, with_out=with_out, q_scale=dk ** -0.5,
                          heads=heads, dk=dk, dv=dv),
        out_shape=out_shape,
        grid=(bsz, nh // heads, n),
        in_specs=in_specs,
        out_specs=out_specs,
        scratch_shapes=[pltpu.VMEM((dk, dv), F32)] * (2 * heads) + [pltpu.VMEM((tb, dk), F32)] * (2 * heads),
        compiler_params=_cparams(3),
        name=name,
    )(*args)


def _gla_out_kernel(of_ref, ob_ref, r_ref, og_ref, w_ref, h_ref, gate_ref, o_ref, *, dv):
    o = of_ref[...].astype(F32) + ob_ref[...].astype(F32)
    heads = []
    for hh in range(GLA_HEADS):
        oh = o[:, hh * dv:(hh + 1) * dv]
        heads.append(oh * lax.rsqrt(jnp.mean(oh * oh, axis=-1, keepdims=True) + EPS))
    o = jnp.concatenate(heads, axis=-1) * og_ref[...]
    y = (o * _silu(r_ref[...].astype(F32))).astype(BF16)
    o_ref[...] = h_ref[...] + gate_ref[...] * _dot(y, w_ref[...])


def _gla_out(of, ob, p, og, w_out, h, mod, *, layer, row_of_tile, tm, dv):
    t, d = h.shape
    n_in = w_out.shape[0]
    return pl.pallas_call(
        functools.partial(_gla_out_kernel, dv=dv),
        out_shape=jax.ShapeDtypeStruct((t, d), F32),
        grid=(t // tm,),
        in_specs=[
            pl.BlockSpec((tm, n_in), lambda i: (i, 0)),
            pl.BlockSpec((tm, n_in), lambda i: (i, 0)),
            pl.BlockSpec((tm, n_in), lambda i: (i, 0)),
            pl.BlockSpec((1, n_in), lambda i: (0, 0)),
            pl.BlockSpec((n_in, d), lambda i: (0, 0), pipeline_mode=pl.Buffered(1)),
            pl.BlockSpec((tm, d), lambda i: (i, 0)),
            _mod_spec(layer, 5, row_of_tile, d),
        ],
        out_specs=pl.BlockSpec((tm, d), lambda i: (i, 0)),
        compiler_params=_cparams(1),
        name="gla_out",
    )(of, ob, p, og, w_out, h, mod)


def _pw1_kernel(z_ref, shift_ref, scale_ref, g_ref, wa_ref, wg_ref, ba_ref, bg_ref, o_ref, u_scr):
    j = pl.program_id(1)

    @pl.when(j == 0)
    def _():
        u_scr[...] = _adanorm(z_ref[...], g_ref[...], shift_ref[...], scale_ref[...]).astype(BF16)

    u = u_scr[...]
    a = _dot(u, wa_ref[...].astype(BF16)) + ba_ref[...]
    gt = _dot(u, wg_ref[...].astype(BF16)) + bg_ref[...]
    o_ref[...] = a * jax.nn.sigmoid(gt)


def _pw1(z, mod, gains, w, b, *, layer, row_of_tile, tm, tn):
    t, d = z.shape
    n_half = w.shape[1] // 2
    nb = n_half // tn
    return pl.pallas_call(
        _pw1_kernel,
        out_shape=jax.ShapeDtypeStruct((t, n_half), F32),
        grid=(t // tm, nb),
        in_specs=[
            pl.BlockSpec((tm, d), lambda i, j: (i, 0)),
            _mod_spec(layer, 3, row_of_tile, d),
            _mod_spec(layer, 4, row_of_tile, d),
            _gain_spec(layer, 1, d),
            pl.BlockSpec((d, tn), lambda i, j: (0, j)),
            pl.BlockSpec((d, tn), lambda i, j: (0, nb + j)),
            pl.BlockSpec((1, tn), lambda i, j: (0, j)),
            pl.BlockSpec((1, tn), lambda i, j: (0, nb + j)),
        ],
        out_specs=pl.BlockSpec((tm, tn), lambda i, j: (i, j)),
        scratch_shapes=[pltpu.VMEM((tm, d), BF16)],
        compiler_params=_cparams(2),
        name="conv_pw1_glu",
    )(z, mod, mod, gains, w, w, b, b)


def _conv_kernel(zp_ref, zc_ref, zn_ref, wdw_ref, bdw_ref, lng_ref, lnb_ref, w2_ref, b2_ref,
                 h_ref, gate_ref, o_ref, ext_scr, dw_scr, *, width, rows_per_pass):
    i = pl.program_id(1)
    tt, d = zc_ref.shape
    halo = CONV_ROW_HALO
    pad = width // 2

    ext_scr[0:halo, :] = jnp.where(i > 0, zp_ref[...], 0.0)
    ext_scr[halo:halo + tt, :] = zc_ref[...]
    ext_scr[halo + tt:, :] = jnp.where(i < pl.num_programs(1) - 1, zn_ref[...], 0.0)

    first = halo - pad

    def strip(cs, carry):
        cols = pl.ds(pl.multiple_of(cs * LANE, LANE), LANE)
        for rb in range(tt // rows_per_pass):
            r0 = rb * rows_per_pass
            acc = jnp.zeros((rows_per_pass, LANE), F32)
            for res in range(SUBLANES):
                part = None
                for s in range(first, first + width):
                    if s % SUBLANES != res:
                        continue
                    src = r0 + s - res
                    term = (ext_scr[src:src + rows_per_pass + SUBLANES, cols]
                            * wdw_ref[s - first:s - first + 1, cols])
                    part = term if part is None else part + term
                acc = acc + part[res:res + rows_per_pass]
            dw_scr[r0:r0 + rows_per_pass, cols] = acc
        return carry

    lax.fori_loop(0, d // LANE, strip, 0)

    zc = dw_scr[...] + bdw_ref[...]
    mu = jnp.mean(zc, axis=-1, keepdims=True)
    zc = zc - mu
    var = jnp.mean(zc * zc, axis=-1, keepdims=True)
    zn = zc * lax.rsqrt(var + EPS) * lng_ref[...] + lnb_ref[...]
    y = _dot(_silu(zn).astype(BF16), w2_ref[...]) + b2_ref[...]
    o_ref[...] = h_ref[...] + gate_ref[...] * y


def _conv(z, wdw, bdw, lng, lnb, w2, b2, h, mod, *, layer, bsz, tt):
    t, d = h.shape
    length = t // bsz
    nt = length // tt
    width = wdw.shape[0]
    hb = tt // CONV_ROW_HALO
    n_halo = length // CONV_ROW_HALO
    z3 = z.reshape(bsz, length, d)
    vec = pl.BlockSpec((1, d), lambda b, i: (0, 0))
    return pl.pallas_call(
        functools.partial(_conv_kernel, width=width, rows_per_pass=128),
        out_shape=jax.ShapeDtypeStruct((bsz, length, d), F32),
        grid=(bsz, nt),
        in_specs=[
            pl.BlockSpec((None, CONV_ROW_HALO, d), lambda b, i: (b, jnp.maximum(i * hb - 1, 0), 0)),
            pl.BlockSpec((None, tt, d), lambda b, i: (b, i, 0)),
            pl.BlockSpec((None, CONV_ROW_HALO, d),
                         lambda b, i: (b, jnp.minimum((i + 1) * hb, n_halo - 1), 0)),
            pl.BlockSpec((width, d), lambda b, i: (0, 0)),
            vec, vec, vec,
            pl.BlockSpec((d, d), lambda b, i: (0, 0), pipeline_mode=pl.Buffered(1)),
            vec,
            pl.BlockSpec((None, tt, d), lambda b, i: (b, i, 0)),
            pl.BlockSpec((None, None, None, 1, d), lambda b, i: (layer, b, 5, 0, 0)),
        ],
        out_specs=pl.BlockSpec((None, tt, d), lambda b, i: (b, i, 0)),
        scratch_shapes=[pltpu.VMEM((tt + 2 * CONV_ROW_HALO, d), F32), pltpu.VMEM((tt, d), F32)],
        compiler_params=_cparams(2),
        name="conv_dw_ln_pw2",
    )(z3, z3, z3, wdw, bdw, lng, lnb, w2, b2, h.reshape(bsz, length, d), mod).reshape(t, d)


def _pos_tables(n_tok, d, dtype):
    rows = n_tok // GRID_W
    quarter = d // 4
    omega = 1.0 / (10000.0 ** (jnp.arange(quarter, dtype=F32) / quarter))
    ar = jnp.arange(rows, dtype=F32)[:, None] * omega[None]
    ac = jnp.arange(GRID_W, dtype=F32)[:, None] * omega[None]
    row_table = jnp.concatenate([jnp.sin(ar), jnp.cos(ar)], axis=-1).astype(dtype)
    col_table = jnp.concatenate([jnp.sin(ac), jnp.cos(ac)], axis=-1).astype(dtype)
    return row_table, col_table


def _pick_tile(n, pref):
    t = min(pref, n)
    while n % t:
        t //= 2
    return t


def kernel(x, c, ctx, c_ctx, ada_w, ada_b, norm_g, final_norm_g, ffn_w_gate, ffn_w_up, ffn_w_down,
           gla_w_in, gla_w_decay_f, gla_b_decay_f, gla_w_decay_b, gla_b_decay_b, gla_out_norm_g,
           gla_w_out, conv_w_pw1, conv_b_pw1, conv_w_dw, conv_b_dw, conv_ln_g, conv_ln_b,
           conv_w_pw2, conv_b_pw2):
    bsz, length, d = x.shape
    ctx_len = ctx.shape[1]
    depth = ada_w.shape[0]
    assert depth == 2 and bsz + 1 <= MOD_ROWS
    f = ffn_w_gate.shape[-1]
    dk_all, dv_all = d // 2, d
    dk, dv = dk_all // GLA_HEADS, dv_all // GLA_HEADS
    rank = gla_w_decay_f.shape[1]
    t_lat, t_ctx = bsz * length, bsz * ctx_len

    gains = norm_g.reshape(depth, 3, 1, d)

    w_in = jnp.transpose(gla_w_in, (0, 2, 1))
    qr = dk_all + dv_all
    assert dv_all == 2 * dk_all
    q_blk, r_blk, k_blk, v_blk = [0], [1, 2], [3], [4, 5]
    n_main = qr + dk_all + dv_all
    v_col, q_col, k_col = dv_all, 2 * dv_all, 2 * dv_all + dk_all
    assert 2 * rank == GATE_COPY
    wz1 = gla_w_in[0, :, n_main:].astype(BF16)
    wz = jnp.concatenate([wz1, wz1, wz1, jnp.zeros((d, LANE - 3 * GATE_COPY), BF16)], axis=1)

    def gate_weight(w, first_row):
        hi = w.astype(BF16)
        lo = (w - hi.astype(F32)).astype(BF16)
        out = jnp.zeros((LANE, w.shape[1]), BF16)
        for copy, part in enumerate((hi, hi, lo)):
            out = lax.dynamic_update_slice(out, part, (copy * GATE_COPY + first_row, 0))
        return out

    wdf = gate_weight(gla_w_decay_f[0], 0)
    wdb = gate_weight(gla_w_decay_b[0], rank)
    bdf = gla_b_decay_f[0].reshape(1, dk_all)
    bdb = gla_b_decay_b[0].reshape(1, dk_all)
    og = gla_out_norm_g[0].reshape(1, dv_all)
    w_out = gla_w_out[0].astype(BF16)

    w_pw1 = conv_w_pw1[0]
    b_pw1 = conv_b_pw1[0].reshape(1, -1)
    w_pw2 = conv_w_pw2[0].astype(BF16)

    cond = jnp.concatenate([c, c_ctx[None], jnp.zeros((MOD_ROWS - bsz - 1, d), F32)], axis=0)
    mod = _modulation(cond, ada_w, ada_b).reshape(depth, MOD_ROWS, N_MOD, 1, d)

    def lat_row(tm):
        return lambda i: i // (length // tm)

    ctx_row = lambda i: bsz

    def ffn(zz, layer, which, tm, row, **kw):
        return _ffn(zz, mod, gains, ffn_w_gate, ffn_w_up, ffn_w_down, layer=layer, which=which,
                    row_of_tile=row, tm=tm, tf=FFN_SLICE, **kw)

    tm_ffn = _pick_tile(length, 1024)
    tm_ctx = _pick_tile(t_ctx, 1024)

    pos = _pos_tables(length, d, x.dtype)
    xf = x.reshape(t_lat, d)
    cf = ctx.reshape(t_ctx, d)

    h = ffn(xf, 0, 0, tm_ffn, lat_row(tm_ffn), pos=pos)
    hc = ffn(cf, 0, 0, tm_ctx, ctx_row)

    p, zg = _inproj(h, mod, gains, w_in, wz, layer=0, row_of_tile=lat_row(tm_ffn), tm=tm_ffn, tn=dk_all,
                    src_blocks=r_blk + v_blk + q_blk + k_blk, name="gla_inproj")
    pc, zgc = _inproj(hc, mod, gains, w_in, wz, layer=0, row_of_tile=ctx_row, tm=_pick_tile(t_ctx, 512),
                      tn=dk_all, src_blocks=v_blk + k_blk, name="gla_inproj_ctx")

    zeros = jnp.zeros((bsz, GLA_HEADS, dk, dv), F32)
    tb_ctx = _pick_tile(ctx_len, GLA_BLOCK)
    sf, sb = _gla_scan(pc.reshape(bsz, ctx_len, -1), zgc.reshape(bsz, ctx_len, -1),
                       wdf, bdf, wdb, bdb, zeros, zeros, with_out=False,
                       q_col=0, k_col=dv_all, v_col=0,
                       dk=dk, dv=dv, tb=tb_ctx, heads=GLA_HEADS, name="gla_scan_ctx")
    tb = _pick_tile(length, GLA_BLOCK)
    of, ob = _gla_scan(p.reshape(bsz, length, -1), zg.reshape(bsz, length, -1),
                       wdf, bdf, wdb, bdb, sf, sb, with_out=True, q_col=q_col, k_col=k_col, v_col=v_col,
                       dk=dk, dv=dv, tb=tb, heads=GLA_HEADS, name="gla_scan")
    tm_o = _pick_tile(length, 512)
    h = _gla_out(of.reshape(t_lat, dv_all), ob.reshape(t_lat, dv_all), p, og, w_out, h, mod,
                 layer=0, row_of_tile=lat_row(tm_o), tm=tm_o, dv=dv)
    h = ffn(h, 0, 1, tm_ffn, lat_row(tm_ffn))

    h = ffn(h, 1, 0, tm_ffn, lat_row(tm_ffn))
    z = _pw1(h, mod, gains, w_pw1, b_pw1, layer=1, row_of_tile=lat_row(tm_ffn), tm=tm_ffn,
             tn=_pick_tile(d, 512))
    h = _conv(z, conv_w_dw[0], conv_b_dw[0].reshape(1, d), conv_ln_g[0].reshape(1, d),
              conv_ln_b[0].reshape(1, d), w_pw2, conv_b_pw2[0].reshape(1, d), h, mod,
              layer=1, bsz=bsz, tt=_pick_tile(length, 256))
    h = ffn(h, 1, 1, tm_ffn, lat_row(tm_ffn), final_gain=final_norm_g.reshape(1, d))
    return h.reshape(bsz, length, d)
```

```python
import functools

import jax
import jax.numpy as jnp
from jax import lax
from jax.experimental import pallas as pl
from jax.experimental.pallas import tpu as pltpu

F32 = jnp.float32
BF16 = jnp.bfloat16

EPS = 1e-6
N_MOD = 9
GRID_W = 64
GLA_HEADS = 4
GLA_GATE_TEMP = 16.0
LOG2_E = 1.4426950408889634
GLA_BLOCK = 256
GLA_CHUNK = 64
GLA_SUB = 16
GATE_COPY = 32
FFN_SLICE = 256
SUBLANES = 8
CONV_ROW_HALO = 16
MOD_ROWS = 8
V7X_VMEM_LIMIT = 60 * 1024 * 1024
LANE = 128


def _cparams(n_axes, flags=None):
    return pltpu.CompilerParams(
        dimension_semantics=("arbitrary",) * n_axes,
        vmem_limit_bytes=V7X_VMEM_LIMIT,
        flags=flags)


def _dot(a, b):
    return jnp.dot(a, b, preferred_element_type=F32)


def _dot_tn(a, b):
    return lax.dot_general(a, b, (((0,), (0,)), ((), ())), preferred_element_type=F32)


def _silu(x):
    return x * jax.nn.sigmoid(x)


def _adanorm(z, g, shift, scale):
    y = z * lax.rsqrt(jnp.mean(z * z, axis=-1, keepdims=True) + EPS)
    return y * (g * (1.0 + scale)) + shift


def _mod_kernel(c_ref, w_ref, b_ref, o_ref):
    s = _silu(c_ref[...]).astype(BF16)
    o_ref[...] = _dot(s, w_ref[...].astype(BF16)) + b_ref[...]


def _modulation(cond, ada_w, ada_b):
    depth, d, n = ada_w.shape
    tn = _pick_tile(n, 1024)
    return pl.pallas_call(
        _mod_kernel,
        out_shape=jax.ShapeDtypeStruct((depth, MOD_ROWS, n), F32),
        grid=(depth, n // tn),
        in_specs=[
            pl.BlockSpec((MOD_ROWS, d), lambda l, j: (0, 0)),
            pl.BlockSpec((None, d, tn), lambda l, j: (l, 0, j)),
            pl.BlockSpec((None, 1, tn), lambda l, j: (l, 0, j)),
        ],
        out_specs=pl.BlockSpec((None, MOD_ROWS, tn), lambda l, j: (l, 0, j)),
        compiler_params=_cparams(2),
        name="ada_modulation",
    )(cond, ada_w, ada_b.reshape(depth, 1, n))


def _mod_spec(layer, k, row_of_tile, d):
    return pl.BlockSpec((None, None, None, 1, d),
                        lambda i, *_: (layer, row_of_tile(i), k, 0, 0))


def _gain_spec(layer, k, d):
    return pl.BlockSpec((None, None, 1, d), lambda *_: (layer, k, 0, 0))


def _ffn_kernel(*refs, has_pos, final_norm, has_tail):
    it = iter(refs)
    z_ref = next(it)
    if has_pos:
        prow_ref, pcol_ref = next(it), next(it)
    shift_ref, scale_ref, gate_ref, g_ref = next(it), next(it), next(it), next(it)
    fg_ref = next(it) if final_norm else None
    wg_ref, wu_ref, wd_ref = next(it), next(it), next(it)
    if has_tail:
        wg_tail, wu_tail, wd_tail = next(it), next(it), next(it)
    o_ref = next(it)
    u_scr = next(it)

    j = pl.program_id(1)

    def residual_in():
        z = z_ref[...]
        if has_pos:
            grid_w, half = pcol_ref.shape
            cols = pcol_ref[...]
            z = z + jnp.concatenate(
                [jnp.concatenate([jnp.broadcast_to(prow_ref[r:r + 1, :], (grid_w, half)), cols], axis=1)
                 for r in range(prow_ref.shape[0])], axis=0)
        return z

    def swiglu_down(u, wg, wu, wd):
        a = _dot(u, wg[...].astype(BF16))
        b = _dot(u, wu[...].astype(BF16))
        return _dot((_silu(a) * b).astype(BF16), wd[...].astype(BF16))

    @pl.when(j == 0)
    def _():
        u = _adanorm(residual_in(), g_ref[...], shift_ref[...], scale_ref[...]).astype(BF16)
        u_scr[...] = u
        o_ref[...] = swiglu_down(u, wg_tail, wu_tail, wd_tail) if has_tail else jnp.zeros_like(o_ref)

    o_ref[...] += swiglu_down(u_scr[...], wg_ref, wu_ref, wd_ref)

    @pl.when(j == pl.num_programs(1) - 1)
    def _():
        out = residual_in() + (0.5 * gate_ref[...]) * o_ref[...]
        if final_norm:
            out = out * lax.rsqrt(jnp.mean(out * out, axis=-1, keepdims=True) + EPS) * fg_ref[...]
        o_ref[...] = out


def _ffn(z, mod, gains, wg, wu, wd, *, layer, which, row_of_tile, tm, tf,
         pos=None, final_gain=None):
    t, d = z.shape
    f = wg.shape[-1]
    tail = f % tf
    n_main = f // tf
    has_tail = tail > 0
    assert tail % LANE == 0 and (not has_tail or (n_main * tf) % tail == 0)
    base = 0 if which == 0 else 6
    gain_k = 0 if which == 0 else 2
    has_pos = pos is not None
    final_norm = final_gain is not None
    once = pl.Buffered(1)

    in_specs = [pl.BlockSpec((tm, d), lambda i, j: (i, 0))]
    args = [z]
    if has_pos:
        row_tab, col_tab = pos
        rows_per_tile = tm // col_tab.shape[0]
        n_pos = row_tab.shape[0] // rows_per_tile
        in_specs += [pl.BlockSpec((rows_per_tile, row_tab.shape[1]), lambda i, j: (i % n_pos, 0)),
                     pl.BlockSpec(col_tab.shape, lambda i, j: (0, 0))]
        args += [row_tab, col_tab]
    for k in range(3):
        in_specs.append(_mod_spec(layer, base + k, row_of_tile, d))
        args.append(mod)
    in_specs.append(_gain_spec(layer, gain_k, d))
    args.append(gains)
    if final_norm:
        in_specs.append(pl.BlockSpec((1, d), lambda i, j: (0, 0)))
        args.append(final_gain)
    in_specs += [
        pl.BlockSpec((None, None, d, tf), lambda i, j: (layer, which, 0, j)),
        pl.BlockSpec((None, None, d, tf), lambda i, j: (layer, which, 0, j)),
        pl.BlockSpec((None, None, tf, d), lambda i, j: (layer, which, j, 0)),
    ]
    args += [wg, wu, wd]
    if has_tail:
        tail_blk = n_main * tf // tail
        in_specs += [
            pl.BlockSpec((None, None, d, tail), lambda i, j: (layer, which, 0, tail_blk), pipeline_mode=once),
            pl.BlockSpec((None, None, d, tail), lambda i, j: (layer, which, 0, tail_blk), pipeline_mode=once),
            pl.BlockSpec((None, None, tail, d), lambda i, j: (layer, which, tail_blk, 0), pipeline_mode=once),
        ]
        args += [wg, wu, wd]

    return pl.pallas_call(
        functools.partial(_ffn_kernel, has_pos=has_pos, final_norm=final_norm, has_tail=has_tail),
        out_shape=jax.ShapeDtypeStruct((t, d), F32),
        grid=(t // tm, n_main),
        in_specs=in_specs,
        out_specs=pl.BlockSpec((tm, d), lambda i, j: (i, 0)),
        scratch_shapes=[pltpu.VMEM((tm, d), BF16)],
        compiler_params=_cparams(2),
        name=f"ffn_l{layer}_{which}",
    )(*args)


def _inproj_kernel(z_ref, shift_ref, scale_ref, g_ref, w_ref, wz_ref, o_ref, oz_ref, u_scr):
    j = pl.program_id(1)

    @pl.when(j == 0)
    def _():
        u = _adanorm(z_ref[...], g_ref[...], shift_ref[...], scale_ref[...]).astype(BF16)
        u_scr[...] = u
        oz_ref[...] = _dot(u, wz_ref[...])

    o_ref[...] = lax.dot_general(u_scr[...], w_ref[...].astype(BF16), (((1,), (1,)), ((), ())),
                                 preferred_element_type=F32).astype(o_ref.dtype)


def _inproj(z, mod, gains, w, wz, *, layer, row_of_tile, tm, tn, src_blocks, name):
    t, d = z.shape
    n_out = len(src_blocks) * tn

    def src_block(j):
        blk = src_blocks[-1]
        for n in range(len(src_blocks) - 2, -1, -1):
            blk = jnp.where(j == n, src_blocks[n], blk)
        return blk

    return pl.pallas_call(
        _inproj_kernel,
        out_shape=(jax.ShapeDtypeStruct((t, n_out), BF16),
                   jax.ShapeDtypeStruct((t, wz.shape[1]), F32)),
        grid=(t // tm, n_out // tn),
        in_specs=[
            pl.BlockSpec((tm, d), lambda i, j: (i, 0)),
            _mod_spec(layer, 3, row_of_tile, d),
            _mod_spec(layer, 4, row_of_tile, d),
            _gain_spec(layer, 1, d),
            pl.BlockSpec((None, tn, d), lambda i, j: (0, src_block(j), 0)),
            pl.BlockSpec(wz.shape, lambda i, j: (0, 0)),
        ],
        out_specs=(pl.BlockSpec((tm, tn), lambda i, j: (i, j)),
                   pl.BlockSpec((tm, wz.shape[1]), lambda i, j: (i, 0))),
        scratch_shapes=[pltpu.VMEM((tm, d), BF16)],
        compiler_params=_cparams(2),
        name=name,
    )(z, mod, mod, gains, w, wz)


def _log_sigmoid(x):
    return jnp.minimum(x, 0.0) - jnp.log(1.0 + jnp.exp(-jnp.abs(x)))


def _split_bf16(x, parts):
    out = []
    for _ in range(parts - 1):
        hi = x.astype(BF16)
        out.append(hi)
        x = x - hi.astype(F32)
    out.append(x.astype(BF16))
    return out


def _gate_features(z):
    lane = lax.broadcasted_iota(jnp.int32, z.shape, 1)
    is_lo = (lane >= GATE_COPY) & (lane < 2 * GATE_COPY)
    return jnp.where(is_lo, z - z.astype(BF16).astype(F32), z).astype(BF16)


def _gate_log_decay(zcat, w, bias):
    return _log_sigmoid(_dot(zcat, w) + bias) * (LOG2_E / GLA_GATE_TEMP)


def _time_tri(t, reverse):
    row = lax.broadcasted_iota(jnp.int32, (t, t), 0)
    col = lax.broadcasted_iota(jnp.int32, (t, t), 1)
    return jnp.where((col >= row) if reverse else (col <= row), 1.0, 0.0).astype(BF16)


def _cumsum_rows(tri, g):
    g_hi, g_lo = _split_bf16(g, 2)
    return _dot(tri, g_hi) + _dot(tri, g_lo)


def _score_masks(t, reverse):
    row = lax.broadcasted_iota(jnp.int32, (GLA_SUB, t), 0)
    col = lax.broadcasted_iota(jnp.int32, (GLA_SUB, t), 1)
    col_sub, col_chunk = col // GLA_SUB, col // GLA_CHUNK
    within = col - col_sub * GLA_SUB
    causal = (within >= row) if reverse else (within <= row)
    return col_sub, col_chunk, causal


def _gla_block(q, k, v, b_ref, st_ref, masks, *, reverse):
    t = b_ref.shape[0]
    n_sub, n_chunk, sub_per_chunk = t // GLA_SUB, t // GLA_CHUNK, GLA_CHUNK // GLA_SUB

    def rows(x, size, i):
        return x[i * size:(i + 1) * size]

    def b_rows(size, i):
        return b_ref[i * size:(i + 1) * size, :]

    def far_edge(size, j):
        r = j * size if reverse else (j + 1) * size - 1
        return b_ref[r:r + 1, :]

    b_end = far_edge(t, 0)
    e_chunk = [far_edge(GLA_CHUNK, j) for j in range(n_chunk)]
    k_chunk = jnp.concatenate(
        [rows(k, GLA_CHUNK, j) * jnp.exp2(e_chunk[j] - b_rows(GLA_CHUNK, j)) for j in range(n_chunk)], axis=0)
    yield

    o = None
    if q is not None:
        o = _dot((q * jnp.exp2(b_ref[...])).astype(BF16), st_ref[...].astype(BF16))
        e_sub = [far_edge(GLA_SUB, j) for j in range(n_sub)]
        k_sub = jnp.concatenate(
            [rows(k, GLA_SUB, j) * jnp.exp2(e_sub[j] - b_rows(GLA_SUB, j)) for j in range(n_sub)], axis=0)

        def visible(i, n):
            return range(i, n) if reverse else range(i + 1)

        yield
        near = [(i, j) for i in range(n_sub) for j in visible(i, n_sub)
                if j // sub_per_chunk == i // sub_per_chunk]
        far = [(i, j) for i in range(n_chunk) for j in visible(i, n_chunk) if j != i]
        lhs_near = jnp.concatenate(
            [rows(q, GLA_SUB, i) * jnp.exp2(b_rows(GLA_SUB, i) - e_sub[j]) for i, j in near], axis=0)
        lhs_far = jnp.concatenate(
            [rows(q, GLA_CHUNK, i) * jnp.exp2(b_rows(GLA_CHUNK, i) - e_chunk[j]) for i, j in far], axis=0)
        yield
        r_near = _dot(lhs_near.astype(BF16), k_sub.astype(BF16).T)
        r_far = _dot(lhs_far.astype(BF16), k_chunk.astype(BF16).T)

        yield
        col_sub, col_chunk, causal = masks
        a_rows = []
        for i in range(n_sub):
            acc = jnp.zeros((GLA_SUB, t), F32)
            for n, (pi, pj) in enumerate(near):
                if pi == i:
                    sel = (col_sub == pj) & causal if pj == i else col_sub == pj
                    acc = jnp.where(sel, rows(r_near, GLA_SUB, n), acc)
            for n, (pi, pj) in enumerate(far):
                if pi == i // sub_per_chunk:
                    piece = rows(rows(r_far, GLA_CHUNK, n), GLA_SUB, i % sub_per_chunk)
                    acc = jnp.where(col_chunk == pj, piece, acc)
            a_rows.append(acc)
        yield
        o = o + _dot(jnp.concatenate(a_rows, axis=0).astype(BF16), v)

    kd = jnp.concatenate(
        [rows(k_chunk, GLA_CHUNK, j) * jnp.exp2(b_end - e_chunk[j]) for j in range(n_chunk)], axis=0)
    decay = jnp.exp2(jnp.broadcast_to(b_end, (LANE, b_end.shape[1])).T)
    st_ref[...] = st_ref[...] * jnp.tile(decay, (1, st_ref.shape[1] // LANE)) + _dot_tn(kd.astype(BF16), v)
    return o


def _gla_scan_kernel(*refs, with_out, q_scale, heads, dk, dv):
    it = iter(refs)
    if with_out:
        qf_ref, qb_ref = next(it), next(it)
    kf_ref, kb_ref, vf_ref, vb_ref, zf_ref, zb_ref = (next(it) for _ in range(6))
    wdf_ref, bdf_ref, wdb_ref, bdb_ref, s0f_ref, s0b_ref = (next(it) for _ in range(6))
    if with_out:
        of_ref, ob_ref = next(it), next(it)
    else:
        sf_out, sb_out = next(it), next(it)
    sf_scr, sb_scr, bf_scr, bb_scr = ([next(it) for _ in range(heads)] for _ in range(4))

    step = pl.program_id(2)
    tb = zf_ref.shape[0]

    @pl.when(step == 0)
    def _():
        for h in range(heads):
            sf_scr[h][...] = s0f_ref[h]
            sb_scr[h][...] = s0b_ref[h]

    def direction(q_ref, k_ref, v_ref, z_ref, wd_ref, bd_ref, b_scr, s_scr, o_ref, reverse):
        zcat = _gate_features(z_ref[...])
        tri = _time_tri(tb, reverse)
        masks = _score_masks(tb, reverse) if with_out else None
        chains = []
        for h in range(heads):
            ck, cv = slice(h * dk, (h + 1) * dk), slice(h * dv, (h + 1) * dv)
            b_scr[h][...] = _cumsum_rows(tri, _gate_log_decay(zcat, wd_ref[:, ck], bd_ref[:, ck]))
            block = _gla_block(q_ref[:, ck].astype(F32) * q_scale if with_out else None,
                               k_ref[:, ck].astype(F32), v_ref[:, cv], b_scr[h], s_scr[h], masks,
                               reverse=reverse)
            chains.append((block, o_ref, cv))
        return chains

    fwd = direction(qf_ref if with_out else None, kf_ref, vf_ref, zf_ref, wdf_ref, bdf_ref, bf_scr, sf_scr,
                    of_ref if with_out else None, False)
    bwd = direction(qb_ref if with_out else None, kb_ref, vb_ref, zb_ref, wdb_ref, bdb_ref, bb_scr, sb_scr,
                    ob_ref if with_out else None, True)
    live = [chain for pair in zip(fwd, bwd) for chain in pair]
    waiting, live = live, []
    while waiting or live:
        if waiting:
            live.append(waiting.pop(0))
        unfinished = []
        for block, o_ref, cv in live:
            try:
                next(block)
                unfinished.append((block, o_ref, cv))
            except StopIteration as done:
                if with_out:
                    o_ref[:, cv] = done.value.astype(o_ref.dtype)
        live = unfinished

    if not with_out:
        @pl.when(step == pl.num_programs(2) - 1)
        def _():
            for h in range(heads):
                sf_out[h] = sf_scr[h][...]
                sb_out[h] = sb_scr[h][...]


def _gla_scan(p, zg, wdf, bdf, wdb, bdb, s0f, s0b, *, with_out, q_col, k_col, v_col,
              dk, dv, tb, heads, name):
    bsz, length, _ = p.shape
    nh = GLA_HEADS
    n = length // tb
    zw = zg.shape[-1]
    wk, wv = heads * dk, heads * dv
    assert nh % heads == 0 and q_col % wk == 0 and k_col % wk == 0 and v_col % wv == 0
    kq0, kk0, kv0 = q_col // wk, k_col // wk, v_col // wv

    def fwd(c):
        return c

    def bwd(c):
        return n - 1 - c

    def tile(width, c0, order):
        return pl.BlockSpec((None, tb, width), lambda b, g, c: (b, order(c), c0 + g))

    state = pl.BlockSpec((None, heads, dk, dv), lambda b, g, c: (b, g, 0, 0))
    in_specs, args = [], []
    if with_out:
        in_specs += [tile(wk, kq0, fwd), tile(wk, kq0, bwd)]
        args += [p, p]
    in_specs += [tile(wk, kk0, fwd), tile(wk, kk0, bwd),
                 tile(wv, kv0, fwd), tile(wv, kv0, bwd),
                 pl.BlockSpec((None, tb, zw), lambda b, g, c: (b, c, 0)),
                 pl.BlockSpec((None, tb, zw), lambda b, g, c: (b, n - 1 - c, 0)),
                 pl.BlockSpec((zw, wk), lambda b, g, c: (0, g)),
                 pl.BlockSpec((1, wk), lambda b, g, c: (0, g)),
                 pl.BlockSpec((zw, wk), lambda b, g, c: (0, g)),
                 pl.BlockSpec((1, wk), lambda b, g, c: (0, g)),
                 state, state]
    args += [p, p, p, p, zg, zg, wdf, bdf, wdb, bdb, s0f, s0b]

    if with_out:
        out_shape = (jax.ShapeDtypeStruct((bsz, length, nh * dv), BF16),) * 2
        out_specs = (pl.BlockSpec((None, tb, wv), lambda b, g, c: (b, c, g)),
                     pl.BlockSpec((None, tb, wv), lambda b, g, c: (b, n - 1 - c, g)))
    else:
        out_shape = (jax.ShapeDtypeStruct((bsz, nh, dk, dv), F32),) * 2
        out_specs = (state, state)

    return pl.pallas_call(
        functools.partial(_gla_scan_kernel, with_out=with_out, q_scale=dk ** -0.5,
                          heads=heads, dk=dk, dv=dv),
        out_shape=out_shape,
        grid=(bsz, nh // heads, n),
        in_specs=in_specs,
        out_specs=out_specs,
        scratch_shapes=[pltpu.VMEM((dk, dv), F32)] * (2 * heads) + [pltpu.VMEM((tb, dk), F32)] * (2 * heads),
        compiler_params=_cparams(3),
        name=name,
    )(*args)


def _gla_out_kernel(of_ref, ob_ref, r_ref, og_ref, w_ref, h_ref, gate_ref, o_ref, *, dv):
    o = of_ref[...].astype(F32) + ob_ref[...].astype(F32)
    heads = []
    for hh in range(GLA_HEADS):
        oh = o[:, hh * dv:(hh + 1) * dv]
        heads.append(oh * lax.rsqrt(jnp.mean(oh * oh, axis=-1, keepdims=True) + EPS))
    o = jnp.concatenate(heads, axis=-1) * og_ref[...]
    y = (o * _silu(r_ref[...].astype(F32))).astype(BF16)
    o_ref[...] = h_ref[...] + gate_ref[...] * _dot(y, w_ref[...])


def _gla_out(of, ob, p, og, w_out, h, mod, *, layer, row_of_tile, tm, dv):
    t, d = h.shape
    n_in = w_out.shape[0]
    return pl.pallas_call(
        functools.partial(_gla_out_kernel, dv=dv),
        out_shape=jax.ShapeDtypeStruct((t, d), F32),
        grid=(t // tm,),
        in_specs=[
            pl.BlockSpec((tm, n_in), lambda i: (i, 0)),
            pl.BlockSpec((tm, n_in), lambda i: (i, 0)),
            pl.BlockSpec((tm, n_in), lambda i: (i, 0)),
            pl.BlockSpec((1, n_in), lambda i: (0, 0)),
            pl.BlockSpec((n_in, d), lambda i: (0, 0), pipeline_mode=pl.Buffered(1)),
            pl.BlockSpec((tm, d), lambda i: (i, 0)),
            _mod_spec(layer, 5, row_of_tile, d),
        ],
        out_specs=pl.BlockSpec((tm, d), lambda i: (i, 0)),
        compiler_params=_cparams(1),
        name="gla_out",
    )(of, ob, p, og, w_out, h, mod)


def _pw1_kernel(z_ref, shift_ref, scale_ref, g_ref, wa_ref, wg_ref, ba_ref, bg_ref, o_ref, u_scr):
    j = pl.program_id(1)

    @pl.when(j == 0)
    def _():
        u_scr[...] = _adanorm(z_ref[...], g_ref[...], shift_ref[...], scale_ref[...]).astype(BF16)

    u = u_scr[...]
    a = _dot(u, wa_ref[...].astype(BF16)) + ba_ref[...]
    gt = _dot(u, wg_ref[...].astype(BF16)) + bg_ref[...]
    o_ref[...] = a * jax.nn.sigmoid(gt)


def _pw1(z, mod, gains, w, b, *, layer, row_of_tile, tm, tn):
    t, d = z.shape
    n_half = w.shape[1] // 2
    nb = n_half // tn
    return pl.pallas_call(
        _pw1_kernel,
        out_shape=jax.ShapeDtypeStruct((t, n_half), F32),
        grid=(t // tm, nb),
        in_specs=[
            pl.BlockSpec((tm, d), lambda i, j: (i, 0)),
            _mod_spec(layer, 3, row_of_tile, d),
            _mod_spec(layer, 4, row_of_tile, d),
            _gain_spec(layer, 1, d),
            pl.BlockSpec((d, tn), lambda i, j: (0, j)),
            pl.BlockSpec((d, tn), lambda i, j: (0, nb + j)),
            pl.BlockSpec((1, tn), lambda i, j: (0, j)),
            pl.BlockSpec((1, tn), lambda i, j: (0, nb + j)),
        ],
        out_specs=pl.BlockSpec((tm, tn), lambda i, j: (i, j)),
        scratch_shapes=[pltpu.VMEM((tm, d), BF16)],
        compiler_params=_cparams(2),
        name="conv_pw1_glu",
    )(z, mod, mod, gains, w, w, b, b)


def _conv_kernel(zp_ref, zc_ref, zn_ref, wdw_ref, bdw_ref, lng_ref, lnb_ref, w2_ref, b2_ref,
                 h_ref, gate_ref, o_ref, ext_scr, dw_scr, *, width, rows_per_pass):
    i = pl.program_id(1)
    tt, d = zc_ref.shape
    halo = CONV_ROW_HALO
    pad = width // 2

    ext_scr[0:halo, :] = jnp.where(i > 0, zp_ref[...], 0.0)
    ext_scr[halo:halo + tt, :] = zc_ref[...]
    ext_scr[halo + tt:, :] = jnp.where(i < pl.num_programs(1) - 1, zn_ref[...], 0.0)

    first = halo - pad

    def strip(cs, carry):
        cols = pl.ds(pl.multiple_of(cs * LANE, LANE), LANE)
        for rb in range(tt // rows_per_pass):
            r0 = rb * rows_per_pass
            acc = jnp.zeros((rows_per_pass, LANE), F32)
            for res in range(SUBLANES):
                part = None
                for s in range(first, first + width):
                    if s % SUBLANES != res:
                        continue
                    src = r0 + s - res
                    term = (ext_scr[src:src + rows_per_pass + SUBLANES, cols]
                            * wdw_ref[s - first:s - first + 1, cols])
                    part = term if part is None else part + term
                acc = acc + part[res:res + rows_per_pass]
            dw_scr[r0:r0 + rows_per_pass, cols] = acc
        return carry

    lax.fori_loop(0, d // LANE, strip, 0)

    zc = dw_scr[...] + bdw_ref[...]
    mu = jnp.mean(zc, axis=-1, keepdims=True)
    zc = zc - mu
    var = jnp.mean(zc * zc, axis=-1, keepdims=True)
    zn = zc * lax.rsqrt(var + EPS) * lng_ref[...] + lnb_ref[...]
    y = _dot(_silu(zn).astype(BF16), w2_ref[...]) + b2_ref[...]
    o_ref[...] = h_ref[...] + gate_ref[...] * y


def _conv(z, wdw, bdw, lng, lnb, w2, b2, h, mod, *, layer, bsz, tt):
    t, d = h.shape
    length = t // bsz
    nt = length // tt
    width = wdw.shape[0]
    hb = tt // CONV_ROW_HALO
    n_halo = length // CONV_ROW_HALO
    z3 = z.reshape(bsz, length, d)
    vec = pl.BlockSpec((1, d), lambda b, i: (0, 0))
    return pl.pallas_call(
        functools.partial(_conv_kernel, width=width, rows_per_pass=128),
        out_shape=jax.ShapeDtypeStruct((bsz, length, d), F32),
        grid=(bsz, nt),
        in_specs=[
            pl.BlockSpec((None, CONV_ROW_HALO, d), lambda b, i: (b, jnp.maximum(i * hb - 1, 0), 0)),
            pl.BlockSpec((None, tt, d), lambda b, i: (b, i, 0)),
            pl.BlockSpec((None, CONV_ROW_HALO, d),
                         lambda b, i: (b, jnp.minimum((i + 1) * hb, n_halo - 1), 0)),
            pl.BlockSpec((width, d), lambda b, i: (0, 0)),
            vec, vec, vec,
            pl.BlockSpec((d, d), lambda b, i: (0, 0), pipeline_mode=pl.Buffered(1)),
            vec,
            pl.BlockSpec((None, tt, d), lambda b, i: (b, i, 0)),
            pl.BlockSpec((None, None, None, 1, d), lambda b, i: (layer, b, 5, 0, 0)),
        ],
        out_specs=pl.BlockSpec((None, tt, d), lambda b, i: (b, i, 0)),
        scratch_shapes=[pltpu.VMEM((tt + 2 * CONV_ROW_HALO, d), F32), pltpu.VMEM((tt, d), F32)],
        compiler_params=_cparams(2),
        name="conv_dw_ln_pw2",
    )(z3, z3, z3, wdw, bdw, lng, lnb, w2, b2, h.reshape(bsz, length, d), mod).reshape(t, d)


def _pos_tables(n_tok, d, dtype):
    rows = n_tok // GRID_W
    quarter = d // 4
    omega = 1.0 / (10000.0 ** (jnp.arange(quarter, dtype=F32) / quarter))
    ar = jnp.arange(rows, dtype=F32)[:, None] * omega[None]
    ac = jnp.arange(GRID_W, dtype=F32)[:, None] * omega[None]
    row_table = jnp.concatenate([jnp.sin(ar), jnp.cos(ar)], axis=-1).astype(dtype)
    col_table = jnp.concatenate([jnp.sin(ac), jnp.cos(ac)], axis=-1).astype(dtype)
    return row_table, col_table


def _pick_tile(n, pref):
    t = min(pref, n)
    while n % t:
        t //= 2
    return t


def kernel(x, c, ctx, c_ctx, ada_w, ada_b, norm_g, final_norm_g, ffn_w_gate, ffn_w_up, ffn_w_down,
           gla_w_in, gla_w_decay_f, gla_b_decay_f, gla_w_decay_b, gla_b_decay_b, gla_out_norm_g,
           gla_w_out, conv_w_pw1, conv_b_pw1, conv_w_dw, conv_b_dw, conv_ln_g, conv_ln_b,
           conv_w_pw2, conv_b_pw2):
    bsz, length, d = x.shape
    ctx_len = ctx.shape[1]
    depth = ada_w.shape[0]
    assert depth == 2 and bsz + 1 <= MOD_ROWS
    f = ffn_w_gate.shape[-1]
    dk_all, dv_all = d // 2, d
    dk, dv = dk_all // GLA_HEADS, dv_all // GLA_HEADS
    rank = gla_w_decay_f.shape[1]
    t_lat, t_ctx = bsz * length, bsz * ctx_len

    gains = norm_g.reshape(depth, 3, 1, d)

    w_in = jnp.transpose(gla_w_in, (0, 2, 1))
    qr = dk_all + dv_all
    assert dv_all == 2 * dk_all
    q_blk, r_blk, k_blk, v_blk = [0], [1, 2], [3], [4, 5]
    n_main = qr + dk_all + dv_all
    v_col, q_col, k_col = dv_all, 2 * dv_all, 2 * dv_all + dk_all
    assert 2 * rank == GATE_COPY
    wz1 = gla_w_in[0, :, n_main:].astype(BF16)
    wz = jnp.concatenate([wz1, wz1, wz1, jnp.zeros((d, LANE - 3 * GATE_COPY), BF16)], axis=1)

    def gate_weight(w, first_row):
        hi = w.astype(BF16)
        lo = (w - hi.astype(F32)).astype(BF16)
        out = jnp.zeros((LANE, w.shape[1]), BF16)
        for copy, part in enumerate((hi, hi, lo)):
            out = lax.dynamic_update_slice(out, part, (copy * GATE_COPY + first_row, 0))
        return out

    wdf = gate_weight(gla_w_decay_f[0], 0)
    wdb = gate_weight(gla_w_decay_b[0], rank)
    bdf = gla_b_decay_f[0].reshape(1, dk_all)
    bdb = gla_b_decay_b[0].reshape(1, dk_all)
    og = gla_out_norm_g[0].reshape(1, dv_all)
    w_out = gla_w_out[0].astype(BF16)

    w_pw1 = conv_w_pw1[0]
    b_pw1 = conv_b_pw1[0].reshape(1, -1)
    w_pw2 = conv_w_pw2[0].astype(BF16)

    cond = jnp.concatenate([c, c_ctx[None], jnp.zeros((MOD_ROWS - bsz - 1, d), F32)], axis=0)
    mod = _modulation(cond, ada_w, ada_b).reshape(depth, MOD_ROWS, N_MOD, 1, d)

    def lat_row(tm):
        return lambda i: i // (length // tm)

    ctx_row = lambda i: bsz

    def ffn(zz, layer, which, tm, row, **kw):
        return _ffn(zz, mod, gains, ffn_w_gate, ffn_w_up, ffn_w_down, layer=layer, which=which,
                    row_of_tile=row, tm=tm, tf=FFN_SLICE, **kw)

    tm_ffn = _pick_tile(length, 1024)
    tm_ctx = _pick_tile(t_ctx, 1024)

    pos = _pos_tables(length, d, x.dtype)
    xf = x.reshape(t_lat, d)
    cf = ctx.reshape(t_ctx, d)

    h = ffn(xf, 0, 0, tm_ffn, lat_row(tm_ffn), pos=pos)
    hc = ffn(cf, 0, 0, tm_ctx, ctx_row)

    p, zg = _inproj(h, mod, gains, w_in, wz, layer=0, row_of_tile=lat_row(tm_ffn), tm=tm_ffn, tn=dk_all,
                    src_blocks=r_blk + v_blk + q_blk + k_blk, name="gla_inproj")
    pc, zgc = _inproj(hc, mod, gains, w_in, wz, layer=0, row_of_tile=ctx_row, tm=_pick_tile(t_ctx, 512),
                      tn=dk_all, src_blocks=v_blk + k_blk, name="gla_inproj_ctx")

    zeros = jnp.zeros((bsz, GLA_HEADS, dk, dv), F32)
    tb_ctx = _pick_tile(ctx_len, GLA_BLOCK)
    sf, sb = _gla_scan(pc.reshape(bsz, ctx_len, -1), zgc.reshape(bsz, ctx_len, -1),
                       wdf, bdf, wdb, bdb, zeros, zeros, with_out=False,
                       q_col=0, k_col=dv_all, v_col=0,
                       dk=dk, dv=dv, tb=tb_ctx, heads=GLA_HEADS, name="gla_scan_ctx")
    tb = _pick_tile(length, GLA_BLOCK)
    of, ob = _gla_scan(p.reshape(bsz, length, -1), zg.reshape(bsz, length, -1),
                       wdf, bdf, wdb, bdb, sf, sb, with_out=True, q_col=q_col, k_col=k_col, v_col=v_col,
                       dk=dk, dv=dv, tb=tb, heads=GLA_HEADS, name="gla_scan")
    tm_o = _pick_tile(length, 512)
    h = _gla_out(of.reshape(t_lat, dv_all), ob.reshape(t_lat, dv_all), p, og, w_out, h, mod,
                 layer=0, row_of_tile=lat_row(tm_o), tm=tm_o, dv=dv)
    h = ffn(h, 0, 1, tm_ffn, lat_row(tm_ffn))

    h = ffn(h, 1, 0, tm_ffn, lat_row(tm_ffn))
    z = _pw1(h, mod, gains, w_pw1, b_pw1, layer=1, row_of_tile=lat_row(tm_ffn), tm=tm_ffn,
             tn=_pick_tile(d, 512))
    h = _conv(z, conv_w_dw[0], conv_b_dw[0].reshape(1, d), conv_ln_g[0].reshape(1, d),
              conv_ln_b[0].reshape(1, d), w_pw2, conv_b_pw2[0].reshape(1, d), h, mod,
              layer=1, bsz=bsz, tt=_pick_tile(length, 512))
    h = ffn(h, 1, 1, tm_ffn, lat_row(tm_ffn), final_gain=final_norm_g.reshape(1, d))
    return h.reshape(bsz, length, d)
```
